```python
import jax, jax.numpy as jnp
from jax import lax
import numpy as np

D_MODEL = 1024
BATCH = 2
SEQ = 8192
DEPTH = 2

HEAD_DIM = 64
A_Q_HEADS = 6
A_KV_HEADS = 2
WINDOW = 128
B_HEADS = 6
MOBA_BLOCK = 256
MOBA_TOPK = 3
MOBA_Q_CHUNK = 64
C_HEADS = 12
FOX_Q_BLOCK = 128
MEM_HEADS = 4
MEM_LEN = 256
EVEN_MIX = (A_Q_HEADS + B_HEADS + MEM_HEADS) * HEAD_DIM
ODD_MIX = (C_HEADS + MEM_HEADS) * HEAD_DIM
EVEN_SPLIT = [A_Q_HEADS * HEAD_DIM, A_KV_HEADS * HEAD_DIM, A_KV_HEADS * HEAD_DIM,
              B_HEADS * HEAD_DIM, B_HEADS * HEAD_DIM, B_HEADS * HEAD_DIM,
              MEM_HEADS * HEAD_DIM, EVEN_MIX]
ODD_SPLIT = [C_HEADS * HEAD_DIM, C_HEADS * HEAD_DIM, C_HEADS * HEAD_DIM, C_HEADS,
             MEM_HEADS * HEAD_DIM, ODD_MIX]
EVEN_IN = sum(EVEN_SPLIT)
ODD_IN = sum(ODD_SPLIT)
N_ALIBI = A_Q_HEADS + B_HEADS
EPS = 1e-6
NEG = -1e30
SCALE = HEAD_DIM ** -0.5

kernel_name = 'hybrid_swa_moba_fox_memory_trunk'


def rms_norm(x, g):
    xf = x.astype(jnp.float32)
    y = xf * lax.rsqrt(jnp.mean(xf * xf, axis=-1, keepdims=True) + EPS)
    return (y * g.astype(jnp.float32)).astype(x.dtype)


def split_cols(h, sizes):
    return jnp.split(h, np.cumsum(sizes)[:-1].tolist(), axis=-1)


def split_heads(t, n):
    return t.reshape(t.shape[0], t.shape[1], n, HEAD_DIM)


def alibi_slopes():
    h = jnp.arange(1, N_ALIBI + 1, dtype=jnp.float32)
    return 2.0 ** (-8.0 * h / N_ALIBI)


def sliding_window_sink_attn(q, k, v, sinks, slopes):
    B, S, HQ, _ = q.shape
    HKV = k.shape[2]
    G = HQ // HKV
    W = WINDOW
    nb = S // W
    qb = q.reshape(B, nb, W, HKV, G, HEAD_DIM)

    def band(t):
        tb = t.reshape(B, nb, W, HKV, HEAD_DIM)
        prev = jnp.pad(tb, ((0, 0), (1, 0), (0, 0), (0, 0), (0, 0)))[:, :-1]
        return jnp.concatenate([prev, tb], axis=2)

    kb, vb = band(k), band(v)
    logits = jnp.einsum('bnqkgd,bnskd->bnkgqs', qb, kb,
                        preferred_element_type=jnp.float32) * SCALE
    qpos = jnp.arange(nb)[:, None] * W + jnp.arange(W)[None, :]
    kpos = jnp.arange(nb)[:, None] * W - W + jnp.arange(2 * W)[None, :]
    dist = qpos[:, :, None] - kpos[:, None, :]
    allowed = (dist >= 0) & (dist < W) & (kpos[:, None, :] >= 0)
    sl = slopes.reshape(HKV, G)[None, None, :, :, None, None]
    logits = logits - sl * dist[None, :, None, None].astype(jnp.float32)
    logits = jnp.where(allowed[None, :, None, None], logits, NEG)
    sink = jnp.broadcast_to(
        sinks.astype(jnp.float32).reshape(HKV, G)[None, None, :, :, None, None],
        logits.shape[:-1] + (1,))
    p = jax.nn.softmax(jnp.concatenate([logits, sink], axis=-1), axis=-1)[..., :-1]
    o = jnp.einsum('bnkgqs,bnskd->bnqkgd', p.astype(v.dtype), vb)
    return o.reshape(B, S, HQ * HEAD_DIM)


def moba_attn(q, k, v, slopes):
    B, S, H, _ = q.shape
    L = MOBA_BLOCK
    nblk = -(-S // L)
    Sp = nblk * L
    pad = ((0, 0), (0, Sp - S), (0, 0), (0, 0))
    q, k, v = [jnp.pad(t, pad).transpose(0, 2, 1, 3) for t in (q, k, v)]
    kblk = k.reshape(B, H, nblk, L, HEAD_DIM)
    vblk = v.reshape(B, H, nblk, L, HEAD_DIM)
    kmean = jnp.mean(kblk.astype(jnp.float32), axis=3)
    gate = jnp.einsum('bhtd,bhnd->bhtn', q.astype(jnp.float32), kmean)
    qblk_id = jnp.arange(Sp) // L
    past = jnp.arange(nblk)[None, :] < qblk_id[:, None]
    gate = jnp.where(past, gate, NEG)
    K = min(MOBA_TOPK, nblk)
    _, sel = lax.top_k(gate, K)
    sel_valid = jnp.arange(K)[None, :] < qblk_id[:, None]

    C = MOBA_Q_CHUNK
    nch = Sp // C
    q_c = q.reshape(B, H, nch, C, HEAD_DIM).transpose(2, 0, 1, 3, 4)
    sel_c = sel.reshape(B, H, nch, C, K).transpose(2, 0, 1, 3, 4)
    valid_c = sel_valid.reshape(nch, C, K)
    qpos_c = jnp.arange(Sp).reshape(nch, C)
    bi = jnp.arange(B)[:, None, None, None]
    hi = jnp.arange(H)[None, :, None, None]
    sl = slopes.astype(jnp.float32)
    offs = jnp.arange(L)

    def chunk_fn(args):
        qc, selc, validc, qposc = args
        own = qposc[0] // L
        ksel = kblk[bi, hi, selc]
        vsel = vblk[bi, hi, selc]
        kown = lax.dynamic_index_in_dim(kblk, own, axis=2, keepdims=False)
        vown = lax.dynamic_index_in_dim(vblk, own, axis=2, keepdims=False)
        s_sel = jnp.einsum('bhcd,bhckld->bhckl', qc, ksel,
                           preferred_element_type=jnp.float32) * SCALE
        dist_sel = (qposc[None, None, :, None, None]
                    - (selc[..., None] * L + offs)).astype(jnp.float32)
        s_sel = s_sel - sl[None, :, None, None, None] * dist_sel
        s_sel = jnp.where(validc[None, None, :, :, None], s_sel, NEG)
        s_own = jnp.einsum('bhcd,bhld->bhcl', qc, kown,
                           preferred_element_type=jnp.float32) * SCALE
        dist_own = qposc[:, None] - (own * L + offs)[None, :]
        s_own = s_own - sl[None, :, None, None] * dist_own.astype(jnp.float32)[None, None]
        s_own = jnp.where((dist_own >= 0)[None, None], s_own, NEG)
        logits = jnp.concatenate([s_sel.reshape(B, H, C, K * L), s_own], axis=-1)
        p = jax.nn.softmax(logits, axis=-1).astype(v.dtype)
        p_sel = p[..., :K * L].reshape(B, H, C, K, L)
        p_own = p[..., K * L:]
        return (jnp.einsum('bhckl,bhckld->bhcd', p_sel, vsel)
                + jnp.einsum('bhcl,bhld->bhcd', p_own, vown))

    o = lax.map(chunk_fn, (q_c, sel_c, valid_c, qpos_c))
    o = o.transpose(1, 0, 3, 2, 4).reshape(B, Sp, H * HEAD_DIM)
    return o[:, :S]


def forgetting_attn(q, k, v, f_logit):
    B, S, H, _ = q.shape
    logf = jax.nn.log_sigmoid(f_logit.astype(jnp.float32))
    c = jnp.cumsum(logf, axis=1)
    cT = c.transpose(0, 2, 1)
    Q = FOX_Q_BLOCK
    nb = S // Q
    qb = q.reshape(B, nb, Q, H, HEAD_DIM).transpose(1, 0, 2, 3, 4)
    cb = c.reshape(B, nb, Q, H).transpose(1, 0, 3, 2)
    kpos = jnp.arange(S)

    def block_fn(args):
        qblk, cblk, n = args
        s = jnp.einsum('bqhd,bshd->bhqs', qblk, k,
                       preferred_element_type=jnp.float32) * SCALE
        s = s + cblk[..., None] - cT[:, :, None, :]
        qpos = n * Q + jnp.arange(Q)
        s = jnp.where((kpos[None, :] <= qpos[:, None])[None, None], s, NEG)
        p = jax.nn.softmax(s, axis=-1).astype(v.dtype)
        return jnp.einsum('bhqs,bshd->bqhd', p, v)

    o = lax.map(block_fn, (qb, cb, jnp.arange(nb)))
    return o.transpose(1, 0, 2, 3, 4).reshape(B, S, H * HEAD_DIM)


def head_rms(t, g):
    tf = t.astype(jnp.float32)
    y = tf * lax.rsqrt(jnp.mean(tf * tf, axis=-1, keepdims=True) + EPS)
    return (y * g.astype(jnp.float32)).astype(t.dtype)


def memory_kv(mem, mem_norm_g, w_mem_kv, k_gain):
    hm = rms_norm(mem, mem_norm_g)
    mk, mv = jnp.split(hm @ w_mem_kv, 2, axis=-1)
    return head_rms(split_heads(mk, MEM_HEADS), k_gain), split_heads(mv, MEM_HEADS)


def memory_cross_attn(q, mk, mv):
    B, S = q.shape[0], q.shape[1]
    s = jnp.einsum('bshd,bmhd->bhsm', q, mk, preferred_element_type=jnp.float32) * SCALE
    p = jax.nn.softmax(s, axis=-1).astype(mv.dtype)
    return jnp.einsum('bhsm,bmhd->bshd', p, mv).reshape(B, S, MEM_HEADS * HEAD_DIM)


def even_layer(x, mem, norm_g, w_in, qk_g, sinks, mem_norm_g, w_mem_kv, w_out):
    h = rms_norm(x, norm_g)
    aq, ak, av, bq, bk, bv, mq, z = split_cols(h @ w_in, EVEN_SPLIT)
    aq = head_rms(split_heads(aq, A_Q_HEADS), qk_g[0])
    ak = head_rms(split_heads(ak, A_KV_HEADS), qk_g[1])
    av = split_heads(av, A_KV_HEADS)
    bq = head_rms(split_heads(bq, B_HEADS), qk_g[2])
    bk = head_rms(split_heads(bk, B_HEADS), qk_g[3])
    bv = split_heads(bv, B_HEADS)
    mq = head_rms(split_heads(mq, MEM_HEADS), qk_g[4])
    mk, mv = memory_kv(mem, mem_norm_g, w_mem_kv, qk_g[5])
    slopes = alibi_slopes()
    ya = sliding_window_sink_attn(aq, ak, av, sinks, slopes[:A_Q_HEADS])
    yb = moba_attn(bq, bk, bv, slopes[A_Q_HEADS:])
    ym = memory_cross_attn(mq, mk, mv)
    y = jnp.concatenate([ya, yb, ym], axis=-1) * jax.nn.silu(z)
    return x + y @ w_out


def odd_layer(x, mem, norm_g, w_in, qk_g, b_f, mem_norm_g, w_mem_kv, w_out):
    h = rms_norm(x, norm_g)
    cq, ck, cv, cf, mq, z = split_cols(h @ w_in, ODD_SPLIT)
    cq = head_rms(split_heads(cq, C_HEADS), qk_g[0])
    ck = head_rms(split_heads(ck, C_HEADS), qk_g[1])
    cv = split_heads(cv, C_HEADS)
    mq = head_rms(split_heads(mq, MEM_HEADS), qk_g[2])
    mk, mv = memory_kv(mem, mem_norm_g, w_mem_kv, qk_g[3])
    yc = forgetting_attn(cq, ck, cv, cf + b_f)
    ym = memory_cross_attn(mq, mk, mv)
    y = jnp.concatenate([yc, ym], axis=-1) * jax.nn.silu(z)
    return x + y @ w_out


def setup_inputs(seed: int = 0) -> dict:
    key = jax.random.key(seed)
    ks = jax.random.split(key, 18)
    NE = (DEPTH + 1) // 2
    NO = DEPTH // 2
    D = D_MODEL
    f32 = jnp.float32

    def w(k, shape, fan_in):
        return jax.random.normal(k, shape, f32) * fan_in ** -0.5

    def gain(k, shape):
        return 1.0 + 0.02 * jax.random.normal(k, shape, f32)

    return {
        'x': jax.random.normal(ks[0], (BATCH, SEQ, D), f32),
        'mem': jax.random.normal(ks[1], (BATCH, MEM_LEN, D), f32),
        'e_norm': gain(ks[2], (NE, D)),
        'e_w_in': w(ks[3], (NE, D, EVEN_IN), D),
        'e_qk_norm': gain(ks[4], (NE, 6, HEAD_DIM)),
        'e_sinks': 0.5 * jax.random.normal(ks[5], (NE, A_Q_HEADS), f32),
        'e_mem_norm': gain(ks[6], (NE, D)),
        'e_w_mem_kv': w(ks[7], (NE, D, 2 * MEM_HEADS * HEAD_DIM), D),
        'e_w_out': w(ks[8], (NE, EVEN_MIX, D), EVEN_MIX),
        'o_norm': gain(ks[9], (NO, D)),
        'o_w_in': w(ks[10], (NO, D, ODD_IN), D),
        'o_qk_norm': gain(ks[11], (NO, 4, HEAD_DIM)),
        'o_b_f': jax.random.uniform(ks[12], (NO, C_HEADS), f32, 1.0, 3.0),
        'o_mem_norm': gain(ks[13], (NO, D)),
        'o_w_mem_kv': w(ks[14], (NO, D, 2 * MEM_HEADS * HEAD_DIM), D),
        'o_w_out': w(ks[15], (NO, ODD_MIX, D), ODD_MIX),
    }


def reference(x, mem, e_norm, e_w_in, e_qk_norm, e_sinks, e_mem_norm, e_w_mem_kv, e_w_out,
              o_norm, o_w_in, o_qk_norm, o_b_f, o_mem_norm, o_w_mem_kv, o_w_out):
    for layer in range(DEPTH):
        i = layer // 2
        if layer % 2 == 0:
            x = even_layer(x, mem, e_norm[i], e_w_in[i], e_qk_norm[i], e_sinks[i],
                           e_mem_norm[i], e_w_mem_kv[i], e_w_out[i])
        else:
            x = odd_layer(x, mem, o_norm[i], o_w_in[i], o_qk_norm[i], o_b_f[i],
                          o_mem_norm[i], o_w_mem_kv[i], o_w_out[i])
    return x
```

```python
import functools

import numpy as np
import jax
import jax.numpy as jnp
from jax import lax
from jax.experimental import pallas as pl
from jax.experimental.pallas import tpu as pltpu

HEAD_DIM = 64
LANES = 128
A_Q_HEADS, A_KV_HEADS, WINDOW = 6, 2, 128
B_HEADS, MOBA_BLOCK, MOBA_TOPK = 6, 256, 3
C_HEADS = 12
MEM_HEADS = 4
N_ALIBI = A_Q_HEADS + B_HEADS
EPS = 1e-6
NEG = -1e30
M_INIT = -1e29
SCALE = HEAD_DIM ** -0.5
VMEM_LIMIT = 56 * 1024 * 1024

A_PERM = (0, 3, 2, 5, 1, 4)

F32 = jnp.float32
BF16 = jnp.bfloat16

_NT = (((1,), (1,)), ((), ()))


def _alibi_slopes():
    h = np.arange(1, N_ALIBI + 1, dtype=np.float32)
    return (np.float32(2.0) ** (np.float32(-8.0) * h / np.float32(N_ALIBI))).astype(np.float32)


def _split3(x):
    hi = x.astype(BF16)
    r1 = x - hi.astype(F32)
    mid = r1.astype(BF16)
    lo = (r1 - mid.astype(F32)).astype(BF16)
    return hi, mid, lo


def _lane_lo(shape_rows=1):
    return lax.broadcasted_iota(jnp.int32, (shape_rows, LANES), 1) < HEAD_DIM


def _proj_body(x_ref, g_ref, w_ref, gain_ref, *out_refs, groups, chunk):
    x = x_ref[...]
    ms = jnp.mean(x * x, axis=-1, keepdims=True)
    h = ((x * lax.rsqrt(ms + EPS)) * g_ref[...]).astype(BF16)
    lo = _lane_lo()
    for out_ref, (col0, width, kind, gain0) in zip(out_refs, groups):
        for c0 in range(0, width, chunk):
            cw = min(chunk, width - c0)
            acc = jnp.dot(h, w_ref[:, col0 + c0:col0 + c0 + cw], preferred_element_type=F32)
            if kind == "norm":
                for b in range(cw // LANES):
                    t = acc[:, b * LANES:(b + 1) * LANES]
                    sq = t * t
                    ss_lo = jnp.sum(jnp.where(lo, sq, 0.0), axis=-1, keepdims=True)
                    ss_hi = jnp.sum(jnp.where(lo, 0.0, sq), axis=-1, keepdims=True)
                    msq = jnp.where(lo, ss_lo, ss_hi) * (1.0 / HEAD_DIM)
                    g0 = gain0 + c0 + b * LANES
                    y = (t * lax.rsqrt(msq + EPS)) * gain_ref[:, g0:g0 + LANES]
                    out_ref[:, c0 + b * LANES:c0 + (b + 1) * LANES] = y.astype(out_ref.dtype)
            else:
                out_ref[:, c0:c0 + cw] = acc.astype(out_ref.dtype)


def _proj(x, g, w, gains, groups, out_dtypes, tm, name):
    t, d = x.shape
    n = w.shape[1]
    out_shape = [jax.ShapeDtypeStruct((t, gw[1]), dt) for gw, dt in zip(groups, out_dtypes)]
    out_specs = [pl.BlockSpec((tm, gw[1]), lambda i: (i, 0)) for gw in groups]
    return pl.pallas_call(
        functools.partial(_proj_body, groups=tuple(groups), chunk=512),
        grid=(t // tm,),
        in_specs=[
            pl.BlockSpec((tm, d), lambda i: (i, 0)),
            pl.BlockSpec((1, d), lambda i: (0, 0)),
            pl.BlockSpec((d, n), lambda i: (0, 0)),
            pl.BlockSpec((1, gains.shape[1]), lambda i: (0, 0)),
        ],
        out_specs=out_specs,
        out_shape=out_shape,
        compiler_params=pltpu.CompilerParams(
            dimension_semantics=("arbitrary",), vmem_limit_bytes=VMEM_LIMIT),
        name=name,
    )(x, g, w, gains)


def _swa_body(sink_ref, q_ref, kp_ref, kc_ref, vp_ref, vc_ref, o_ref, *, tq, slopes):
    i = pl.program_id(1)
    k = jnp.concatenate([kp_ref[...], kc_ref[...]], axis=0)
    v = jnp.concatenate([vp_ref[...], vc_ref[...]], axis=0)
    nk = tq + WINDOW
    row = lax.broadcasted_iota(jnp.int32, (tq, nk), 0)
    col = lax.broadcasted_iota(jnp.int32, (tq, nk), 1)
    dist = row - col + WINDOW
    allowed = (dist >= 0) & (dist < WINDOW) & ((col >= WINDOW) | (i > 0))
    distf = dist.astype(F32)
    lo = _lane_lo()
    for j in range(A_Q_HEADS // 2):
        qb = q_ref[:, j * LANES:(j + 1) * LANES]
        halves = []
        for hh in range(2):
            head = A_PERM[2 * j + hh]
            qm = jnp.where(lo if hh == 0 else jnp.logical_not(lo), qb, jnp.zeros_like(qb))
            s = lax.dot_general(qm, k, _NT, preferred_element_type=F32)
            s = jnp.where(allowed, s - slopes[head] * distf, NEG)
            sink = sink_ref[head]
            m = jnp.maximum(jnp.max(s, axis=-1, keepdims=True), sink)
            e = jnp.exp(s - m)
            den = jnp.sum(e, axis=-1, keepdims=True) + jnp.exp(sink - m)
            p = (e / den).astype(BF16)
            halves.append(jnp.dot(p, v, preferred_element_type=F32))
        o_ref[:, j * LANES:(j + 1) * LANES] = jnp.where(lo, halves[0], halves[1]).astype(o_ref.dtype)


def _swa(aq, ak, av, sinks, batch, seq, tq=256):
    t = aq.shape[0]
    nq = seq // tq
    r = tq // WINDOW
    nb = seq // WINDOW
    slopes = tuple(float(s) for s in _alibi_slopes()[:A_Q_HEADS])
    prev_map = lambda b, i: (jnp.maximum(b * nb + r * i - 1, b * nb), 0)
    cur_map = lambda b, i: (b * nq + i, 0)
    return pl.pallas_call(
        functools.partial(_swa_body, tq=tq, slopes=slopes),
        grid=(batch, nq),
        in_specs=[
            pl.BlockSpec(memory_space=pltpu.SMEM),
            pl.BlockSpec((tq, A_Q_HEADS * HEAD_DIM), cur_map),
            pl.BlockSpec((WINDOW, LANES), prev_map),
            pl.BlockSpec((tq, LANES), cur_map),
            pl.BlockSpec((WINDOW, LANES), prev_map),
            pl.BlockSpec((tq, LANES), cur_map),
        ],
        out_specs=pl.BlockSpec((tq, A_Q_HEADS * HEAD_DIM), cur_map),
        out_shape=jax.ShapeDtypeStruct((t, A_Q_HEADS * HEAD_DIM), BF16),
        compiler_params=pltpu.CompilerParams(
            dimension_semantics=("arbitrary", "arbitrary"), vmem_limit_bytes=VMEM_LIMIT),
        name="swa_attn",
    )(sinks, aq, ak, ak, av, av)


def _flash_step(qa_ref, m_ref, acc_ref, ka, va, mask):
    tk = ka.shape[0]
    for h in range(2):
        s = lax.dot_general(qa_ref[h], ka, _NT, preferred_element_type=F32)
        if mask is not None:
            s = jnp.where(mask, s, NEG)
        m_prev = m_ref[h]
        m_next = jnp.maximum(m_prev, jnp.max(s, axis=-1, keepdims=True))
        alpha = jnp.exp(m_prev - m_next)
        p = jnp.exp(s - pltpu.repeat(m_next, tk // LANES, axis=1)).astype(BF16)
        pv = jnp.dot(p, va, preferred_element_type=F32)
        acc_ref[h] = pltpu.repeat(alpha, 2, axis=1) * acc_ref[h] + pv
        m_ref[h] = m_next


def _flash_init(m_ref, acc_ref):
    m_ref[...] = jnp.full(m_ref.shape, M_INIT, F32)
    acc_ref[...] = jnp.zeros(acc_ref.shape, F32)


def _flash_finish(acc_ref, o_ref):
    outs = [acc_ref[h][:, :LANES] / acc_ref[h][:, LANES:] for h in range(2)]
    o_ref[...] = jnp.where(_lane_lo(), outs[0], outs[1]).astype(o_ref.dtype)


def _chunk_operands(k_ref, ks_ref, v_ref, c, tk):
    off = pl.multiple_of(c * tk, tk)
    ka = jnp.concatenate([k_ref[pl.ds(off, tk), :], ks_ref[pl.ds(off, tk), :]], axis=1)
    va = jnp.concatenate([v_ref[pl.ds(off, tk), :], jnp.ones((tk, LANES), BF16)], axis=1)
    return ka, va


FOX_SIDE_LANES = 6


def _fox_body(q_ref, qs_ref, k_ref, ks_ref, v_ref, o_ref, qa_ref, m_ref, acc_ref, *, tq):
    i = pl.program_id(2)
    lane = lax.broadcasted_iota(jnp.int32, (1, LANES), 1)
    q = q_ref[...]
    qs = qs_ref[...]
    for h in range(2):
        qm = jnp.where((lane < HEAD_DIM) == (h == 0), q, jnp.zeros_like(q))
        side = (lane >= h * FOX_SIDE_LANES) & (lane < (h + 1) * FOX_SIDE_LANES)
        qa_ref[h] = jnp.concatenate([qm, jnp.where(side, qs, jnp.zeros_like(qs))], axis=1)
    _flash_init(m_ref, acc_ref)

    def body(c, carry):
        ka, va = _chunk_operands(k_ref, ks_ref, v_ref, c, tq)
        _flash_step(qa_ref, m_ref, acc_ref, ka, va, None)
        return carry

    lax.fori_loop(0, i, body, 0)
    row = lax.broadcasted_iota(jnp.int32, (tq, tq), 0)
    col = lax.broadcasted_iota(jnp.int32, (tq, tq), 1)
    ka, va = _chunk_operands(k_ref, ks_ref, v_ref, i, tq)
    _flash_step(qa_ref, m_ref, acc_ref, ka, va, col <= row)
    _flash_finish(acc_ref, o_ref)


def _fox(cq, qside, ck, kside, cv, batch, seq, tq=512):
    t, c = cq.shape
    nq = seq // tq
    q_map = lambda b, p, i: (b * nq + i, p)
    kv_map = lambda b, p, i: (b, p)
    return pl.pallas_call(
        functools.partial(_fox_body, tq=tq),
        grid=(batch, c // LANES, nq),
        in_specs=[
            pl.BlockSpec((tq, LANES), q_map),
            pl.BlockSpec((tq, LANES), q_map),
            pl.BlockSpec((seq, LANES), kv_map),
            pl.BlockSpec((seq, LANES), kv_map),
            pl.BlockSpec((seq, LANES), kv_map),
        ],
        out_specs=pl.BlockSpec((tq, LANES), q_map),
        out_shape=jax.ShapeDtypeStruct((t, c), BF16),
        scratch_shapes=[
            pltpu.VMEM((2, tq, 2 * LANES), BF16),
            pltpu.VMEM((2, tq, LANES), F32),
            pltpu.VMEM((2, tq, 2 * LANES), F32),
        ],
        compiler_params=pltpu.CompilerParams(
            dimension_semantics=("arbitrary", "arbitrary", "arbitrary"),
            vmem_limit_bytes=VMEM_LIMIT),
        name="fox_attn",
    )(cq, qside, ck, kside, cv)


def _fox_decay_body(cf_ref, bf_ref, tri_ref, pq_ref, pk_ref, cq_ref, ck_ref,
                    qs_ref, ks_ref, carry_ref):
    @pl.when(pl.program_id(1) == 0)
    def _():
        carry_ref[...] = jnp.zeros(carry_ref.shape, F32)

    x = cf_ref[...] + bf_ref[...]
    logf = -(jnp.maximum(-x, 0.0) + jnp.log1p(jnp.exp(-jnp.abs(x))))
    tri = tri_ref[...]
    c = carry_ref[...]
    for piece in _split3(logf):
        c = c + jnp.dot(tri, piece, preferred_element_type=F32)
    carry_ref[...] = c[c.shape[0] - 1:, :]
    qs = cq_ref[...]
    ks = ck_ref[...]
    for n, piece in enumerate(_split3(c)):
        qs = qs + jnp.dot(piece, pq_ref[n], preferred_element_type=F32)
        ks = ks + jnp.dot(piece, pk_ref[n], preferred_element_type=F32)
    qs_ref[...] = qs.astype(BF16)
    ks_ref[...] = ks.astype(BF16)


def _fox_side_tables():
    width = C_HEADS // 2 * LANES
    pq = np.zeros((3, LANES, width), np.float32)
    pk = np.zeros((3, LANES, width), np.float32)
    cq = np.zeros((1, width), np.float32)
    ck = np.zeros((1, width), np.float32)
    for head in range(C_HEADS):
        base = (head // 2) * LANES + (head % 2) * FOX_SIDE_LANES
        for n in range(3):
            pq[n, head, base + n] = 1.0
            ck[0, base + n] = 1.0
            cq[0, base + 3 + n] = 1.0
            pk[n, head, base + 3 + n] = -1.0
    return (jnp.asarray(pq, BF16), jnp.asarray(pk, BF16), jnp.asarray(cq), jnp.asarray(ck))


def _fox_decay(cf, b_f, batch, seq, rows=256):
    t = cf.shape[0]
    width = C_HEADS // 2 * LANES
    nr = seq // rows
    bf_row = jnp.zeros((1, LANES), F32).at[0, :C_HEADS].set(b_f)
    tri = jnp.asarray(np.tril(np.ones((rows, rows), np.float32)), BF16)
    pq, pk, cq, ck = _fox_side_tables()
    row_map = lambda b, r: (b * nr + r, 0)
    const2 = lambda b, r: (0, 0)
    const3 = lambda b, r: (0, 0, 0)
    return pl.pallas_call(
        _fox_decay_body,
        grid=(batch, nr),
        in_specs=[
            pl.BlockSpec((rows, LANES), row_map),
            pl.BlockSpec((1, LANES), const2),
            pl.BlockSpec((rows, rows), const2),
            pl.BlockSpec((3, LANES, width), const3),
            pl.BlockSpec((3, LANES, width), const3),
            pl.BlockSpec((1, width), const2),
            pl.BlockSpec((1, width), const2),
        ],
        out_specs=[pl.BlockSpec((rows, width), row_map), pl.BlockSpec((rows, width), row_map)],
        out_shape=[jax.ShapeDtypeStruct((t, width), BF16), jax.ShapeDtypeStruct((t, width), BF16)],
        scratch_shapes=[pltpu.VMEM((1, LANES), F32)],
        compiler_params=pltpu.CompilerParams(
            dimension_semantics=("arbitrary", "arbitrary"), vmem_limit_bytes=VMEM_LIMIT),
        name="fox_decay",
    )(cf, bf_row, tri, pq, pk, cq, ck)


MOBA_MAX_BLOCKS = 32
_L_POS, _L_BLK, _L_ONE = MOBA_MAX_BLOCKS, MOBA_MAX_BLOCKS + 3, MOBA_MAX_BLOCKS + 6


def _moba_ktab(seq):
    tab = np.zeros((seq, LANES), np.float32)
    s = np.arange(seq)
    tab[s, s // MOBA_BLOCK] = 1.0
    for n in range(3):
        tab[:, _L_POS + n] = s % MOBA_BLOCK
        tab[:, _L_BLK + n] = s // MOBA_BLOCK
        tab[:, _L_ONE + n] = 1.0
    return jnp.asarray(tab, BF16)


def _moba_body(slope_ref, q_ref, k_ref, kt_ref, v_ref, km_ref, o_ref,
               qa_ref, m_ref, acc_ref, *, tk):
    pair = pl.program_id(1)
    i = pl.program_id(2)
    tq = MOBA_BLOCK
    lane = lax.broadcasted_iota(jnp.int32, (1, LANES), 1)
    q = q_ref[...]
    km_pieces = _split3(km_ref[0])
    blk = lax.broadcasted_iota(jnp.int32, (MOBA_MAX_BLOCKS, tq), 0)
    blkf = blk.astype(F32)
    past = blk < i
    r16 = lax.broadcasted_iota(jnp.int32, (16, tq), 0)
    t_abs = (i * tq + lax.broadcasted_iota(jnp.int32, (1, tq), 1)).astype(F32)
    for h in range(2):
        qm = jnp.where((lane < HEAD_DIM) == (h == 0), q, jnp.zeros_like(q))
        gate = jnp.zeros((MOBA_MAX_BLOCKS, tq), F32)
        for piece in km_pieces:
            gate = gate + lax.dot_general(piece, qm, _NT, preferred_element_type=F32)
        g = jnp.where(past, gate, NEG)
        sel = jnp.zeros((MOBA_MAX_BLOCKS, tq), jnp.bool_)
        for _ in range(MOBA_TOPK):
            mx = jnp.max(g, axis=0, keepdims=True)
            first = jnp.min(jnp.where(g == mx, blkf, 1e9), axis=0, keepdims=True)
            pick = blkf == first
            sel = sel | pick
            g = jnp.where(pick, -jnp.inf, g)
        sel = (sel & past) | (blk == i)
        selbias = jnp.where(sel, 0.0, NEG)
        slope = jnp.full((1, tq), slope_ref[2 * pair + h], F32)
        rows = _split3(slope) + _split3(slope * MOBA_BLOCK) + _split3(-(slope * t_abs))
        extra = jnp.zeros((16, tq), F32)
        for n, rv in enumerate(rows):
            extra = jnp.where(r16 == n, rv.astype(F32), extra)
        side_t = jnp.concatenate(
            [selbias, extra, jnp.zeros((LANES - MOBA_MAX_BLOCKS - 16, tq), F32)], axis=0)
        qa_ref[h] = jnp.concatenate([qm, side_t.T.astype(BF16)], axis=1)
    _flash_init(m_ref, acc_ref)

    def body(c, carry):
        ka, va = _chunk_operands(k_ref, kt_ref, v_ref, c, tk)
        _flash_step(qa_ref, m_ref, acc_ref, ka, va, None)
        return carry

    r = tk // tq
    n_main = i // r
    lax.fori_loop(0, n_main, body, 0)
    row = lax.broadcasted_iota(jnp.int32, (tq, tk), 0)
    col = lax.broadcasted_iota(jnp.int32, (tq, tk), 1)
    ka, va = _chunk_operands(k_ref, kt_ref, v_ref, n_main, tk)
    _flash_step(qa_ref, m_ref, acc_ref, ka, va, (col - row) <= (i - n_main * r) * tq)
    _flash_finish(acc_ref, o_ref)


def _moba(bq, bk, bv, kmean, batch, seq, tk=512):
    t, c = bq.shape
    tq = MOBA_BLOCK
    nq = seq // tq
    assert nq <= MOBA_MAX_BLOCKS and seq % tk == 0
    slopes = jnp.asarray(_alibi_slopes()[A_Q_HEADS:])
    ktab = _moba_ktab(seq)
    q_map = lambda b, p, i: (b * nq + i, p)
    kv_map = lambda b, p, i: (b, p)
    return pl.pallas_call(
        functools.partial(_moba_body, tk=tk),
        grid=(batch, c // LANES, nq),
        in_specs=[
            pl.BlockSpec(memory_space=pltpu.SMEM),
            pl.BlockSpec((tq, LANES), q_map),
            pl.BlockSpec((seq, LANES), kv_map),
            pl.BlockSpec((seq, LANES), lambda b, p, i: (0, 0)),
            pl.BlockSpec((seq, LANES), kv_map),
            pl.BlockSpec((1, MOBA_MAX_BLOCKS, LANES), lambda b, p, i: (b, 0, p)),
        ],
        out_specs=pl.BlockSpec((tq, LANES), q_map),
        out_shape=jax.ShapeDtypeStruct((t, c), BF16),
        scratch_shapes=[
            pltpu.VMEM((2, tq, 2 * LANES), BF16),
            pltpu.VMEM((2, tq, LANES), F32),
            pltpu.VMEM((2, tq, 2 * LANES), F32),
        ],
        compiler_params=pltpu.CompilerParams(
            dimension_semantics=("arbitrary", "arbitrary", "arbitrary"),
            vmem_limit_bytes=VMEM_LIMIT),
        name="moba_attn",
    )(slopes, bq, bk, ktab, bv, kmean)


def _kmean_body(k_ref, o_ref, *, nblk):
    for j in range(nblk):
        kb = k_ref[j * MOBA_BLOCK:(j + 1) * MOBA_BLOCK, :].astype(F32)
        o_ref[0, j:j + 1, :] = jnp.mean(kb, axis=0, keepdims=True)
    if nblk < MOBA_MAX_BLOCKS:
        o_ref[0, nblk:, :] = jnp.zeros((MOBA_MAX_BLOCKS - nblk, o_ref.shape[2]), F32)


def _kmean(bk, batch, seq):
    c = bk.shape[1]
    nblk = seq // MOBA_BLOCK
    return pl.pallas_call(
        functools.partial(_kmean_body, nblk=nblk),
        grid=(batch,),
        in_specs=[pl.BlockSpec((seq, c), lambda b: (b, 0))],
        out_specs=pl.BlockSpec((1, MOBA_MAX_BLOCKS, c), lambda b: (b, 0, 0)),
        out_shape=jax.ShapeDtypeStruct((batch, MOBA_MAX_BLOCKS, c), F32),
        compiler_params=pltpu.CompilerParams(
            dimension_semantics=("arbitrary",), vmem_limit_bytes=VMEM_LIMIT),
        name="moba_kmean",
    )(bk)


def _out_body(*refs, n_parts):
    x_ref = refs[0]
    part_refs = refs[1:1 + n_parts]
    mq_ref, z_ref, mk_ref, mv_ref, w_ref, o_ref, y_ref = refs[1 + n_parts:]
    lo = _lane_lo()
    col = 0
    for pr in part_refs:
        w = pr.shape[1]
        zz = z_ref[:, col:col + w].astype(F32)
        y_ref[:, col:col + w] = (pr[...].astype(F32) * (zz / (1.0 + jnp.exp(-zz)))).astype(BF16)
        col += w
    for pp in range(MEM_HEADS // 2):
        sl = slice(pp * LANES, (pp + 1) * LANES)
        q = mq_ref[:, sl]
        k = mk_ref[:, sl]
        v = mv_ref[:, sl]
        halves = []
        for hh in range(2):
            qm = jnp.where(lo if hh == 0 else jnp.logical_not(lo), q, jnp.zeros_like(q))
            s = lax.dot_general(qm, k, _NT, preferred_element_type=F32)
            e = jnp.exp(s - jnp.max(s, axis=-1, keepdims=True))
            p = (e / jnp.sum(e, axis=-1, keepdims=True)).astype(BF16)
            halves.append(jnp.dot(p, v, preferred_element_type=F32))
        ym = jnp.where(lo, halves[0], halves[1])
        zz = z_ref[:, col + pp * LANES:col + (pp + 1) * LANES].astype(F32)
        y_ref[:, col + pp * LANES:col + (pp + 1) * LANES] = (ym * (zz / (1.0 + jnp.exp(-zz)))).astype(BF16)
    o_ref[...] = x_ref[...] + jnp.dot(y_ref[...], w_ref[...], preferred_element_type=F32)


def _out_proj(x, parts, mq, z, mk, mv, w_out, seq, tm=512):
    t, d = x.shape
    mem_len = mk.shape[0] // (t // seq)
    per_b = seq // tm
    row_map = lambda i: (i, 0)
    mem_map = lambda i: (i // per_b, 0)
    mix = z.shape[1]
    return pl.pallas_call(
        functools.partial(_out_body, n_parts=len(parts)),
        grid=(t // tm,),
        in_specs=[pl.BlockSpec((tm, d), row_map)]
        + [pl.BlockSpec((tm, p.shape[1]), row_map) for p in parts]
        + [
            pl.BlockSpec((tm, mq.shape[1]), row_map),
            pl.BlockSpec((tm, mix), row_map),
            pl.BlockSpec((mem_len, mk.shape[1]), mem_map),
            pl.BlockSpec((mem_len, mv.shape[1]), mem_map),
            pl.BlockSpec((mix, d), lambda i: (0, 0)),
        ],
        out_specs=pl.BlockSpec((tm, d), row_map),
        out_shape=jax.ShapeDtypeStruct((t, d), F32),
        scratch_shapes=[pltpu.VMEM((tm, mix), BF16)],
        compiler_params=pltpu.CompilerParams(
            dimension_semantics=("arbitrary",), vmem_limit_bytes=VMEM_LIMIT),
        name="out_proj",
    )(x, *parts, mq, z, mk, mv, w_out)


def _head_gain(g, heads, scale=1.0):
    return jnp.tile(g.astype(F32), heads) * scale


def _memory_kv(mem2, mem_norm_g, w_mem_kv, k_gain):
    d = mem2.shape[1]
    w = MEM_HEADS * HEAD_DIM
    gains = _head_gain(k_gain, MEM_HEADS)[None, :]
    mk, mv = _proj(mem2, mem_norm_g[None, :].astype(F32), w_mem_kv.astype(BF16), gains,
                   [(0, w, "norm", 0), (w, w, "plain", 0)], [BF16, BF16], tm=256, name="mem_kv")
    return mk, mv


def _perm_heads(w, perm, axis):
    shp = w.shape
    n = shp[axis] // HEAD_DIM
    w = w.reshape(shp[:axis] + (n, HEAD_DIM) + shp[axis + 1:])
    w = jnp.take(w, jnp.asarray(perm), axis=axis)
    return w.reshape(shp)


def _even_layer(x2, mem2, batch, seq, norm_g, w_in, qk_g, sinks, mem_norm_g, w_mem_kv, w_out):
    aw, kvw, bw, mw = A_Q_HEADS * HEAD_DIM, A_KV_HEADS * HEAD_DIM, B_HEADS * HEAD_DIM, MEM_HEADS * HEAD_DIM
    mix = aw + bw + mw
    o = np.cumsum([0, aw, kvw, kvw, bw, bw, bw, mw, mix])
    w = jnp.concatenate([
        _perm_heads(w_in[:, o[0]:o[1]], A_PERM, 1),
        w_in[:, o[1]:o[7]],
        _perm_heads(w_in[:, o[7]:o[7] + aw], A_PERM, 1),
        w_in[:, o[7] + aw:],
    ], axis=1).astype(BF16)
    gains = jnp.concatenate([
        _head_gain(qk_g[0], A_Q_HEADS, SCALE), _head_gain(qk_g[1], A_KV_HEADS),
        _head_gain(qk_g[2], B_HEADS, SCALE), _head_gain(qk_g[3], B_HEADS),
        _head_gain(qk_g[4], MEM_HEADS, SCALE)])[None, :]
    go = np.cumsum([0, aw, kvw, bw, bw])
    groups = [
        (int(o[0]), aw, "norm", int(go[0])), (int(o[1]), kvw, "norm", int(go[1])),
        (int(o[2]), kvw, "plain", 0),
        (int(o[3]), bw, "norm", int(go[2])), (int(o[4]), bw, "norm", int(go[3])),
        (int(o[5]), bw, "plain", 0),
        (int(o[6]), mw, "norm", int(go[4])), (int(o[7]), mix, "plain", 0)]
    aq, ak, av, bq, bk, bv, mq, z = _proj(
        x2, norm_g[None, :].astype(F32), w, gains, groups, [BF16] * 8, tm=512, name="even_in_proj")
    mk, mv = _memory_kv(mem2, mem_norm_g, w_mem_kv, qk_g[5])
    ya = _swa(aq, ak, av, sinks.astype(F32), batch, seq)
    yb = _moba(bq, bk, bv, _kmean(bk, batch, seq), batch, seq)
    w_o = jnp.concatenate([_perm_heads(w_out[:aw], A_PERM, 0), w_out[aw:]], axis=0).astype(BF16)
    return _out_proj(x2, [ya, yb], mq, z, mk, mv, w_o, seq)


def _odd_layer(x2, mem2, batch, seq, norm_g, w_in, qk_g, b_f, mem_norm_g, w_mem_kv, w_out):
    cw, mw = C_HEADS * HEAD_DIM, MEM_HEADS * HEAD_DIM
    mix = cw + mw
    o = np.cumsum([0, cw, cw, cw, C_HEADS, mw, mix])
    d = w_in.shape[0]
    w = jnp.concatenate([
        w_in[:, o[0]:o[4]], jnp.zeros((d, LANES - C_HEADS), w_in.dtype), w_in[:, o[4]:]],
        axis=1).astype(BF16)
    po = np.cumsum([0, cw, cw, cw, LANES, mw, mix])
    gains = jnp.concatenate([
        _head_gain(qk_g[0], C_HEADS, SCALE), _head_gain(qk_g[1], C_HEADS),
        _head_gain(qk_g[2], MEM_HEADS, SCALE)])[None, :]
    groups = [
        (int(po[0]), cw, "norm", 0), (int(po[1]), cw, "norm", cw), (int(po[2]), cw, "plain", 0),
        (int(po[3]), LANES, "plain", 0), (int(po[4]), mw, "norm", 2 * cw), (int(po[5]), mix, "plain", 0)]
    cq, ck, cv, cf, mq, z = _proj(
        x2, norm_g[None, :].astype(F32), w, gains, groups, [BF16, BF16, BF16, F32, BF16, BF16],
        tm=512, name="odd_in_proj")
    mk, mv = _memory_kv(mem2, mem_norm_g, w_mem_kv, qk_g[3])
    qside, kside = _fox_decay(cf, b_f.astype(F32), batch, seq)
    yc = _fox(cq, qside, ck, kside, cv, batch, seq)
    return _out_proj(x2, [yc], mq, z, mk, mv, w_out.astype(BF16), seq)


def kernel(x, mem, e_norm, e_w_in, e_qk_norm, e_sinks, e_mem_norm, e_w_mem_kv, e_w_out,
           o_norm, o_w_in, o_qk_norm, o_b_f, o_mem_norm, o_w_mem_kv, o_w_out):
    batch, seq, d = x.shape
    x2 = x.reshape(batch * seq, d)
    mem2 = mem.reshape(batch * mem.shape[1], d)
    depth = e_norm.shape[0] + o_norm.shape[0]
    for layer in range(depth):
        i = layer // 2
        if layer % 2 == 0:
            x2 = _even_layer(x2, mem2, batch, seq, e_norm[i], e_w_in[i], e_qk_norm[i], e_sinks[i],
                             e_mem_norm[i], e_w_mem_kv[i], e_w_out[i])
        else:
            x2 = _odd_layer(x2, mem2, batch, seq, o_norm[i], o_w_in[i], o_qk_norm[i], o_b_f[i],
                            o_mem_norm[i], o_w_mem_kv[i], o_w_out[i])
    return x2.reshape(batch, seq, d)
```

```python
import functools

import numpy as np
import jax
import jax.numpy as jnp
from jax import lax
from jax.experimental import pallas as pl
from jax.experimental.pallas import tpu as pltpu

HEAD_DIM = 64
LANES = 128
A_Q_HEADS, A_KV_HEADS, WINDOW = 6, 2, 128
B_HEADS, MOBA_BLOCK, MOBA_TOPK = 6, 256, 3
C_HEADS = 12
MEM_HEADS = 4
N_ALIBI = A_Q_HEADS + B_HEADS
EPS = 1e-6
NEG = -1e30
M_INIT = -1e29
SCALE = HEAD_DIM ** -0.5
VMEM_LIMIT = 56 * 1024 * 1024

A_PERM = (0, 3, 2, 5, 1, 4)

F32 = jnp.float32
BF16 = jnp.bfloat16

_NT = (((1,), (1,)), ((), ()))


def _alibi_slopes():
    h = np.arange(1, N_ALIBI + 1, dtype=np.float32)
    return (np.float32(2.0) ** (np.float32(-8.0) * h / np.float32(N_ALIBI))).astype(np.float32)


def _split3(x):
    hi = x.astype(BF16)
    r1 = x - hi.astype(F32)
    mid = r1.astype(BF16)
    lo = (r1 - mid.astype(F32)).astype(BF16)
    return hi, mid, lo


def _lane_lo(shape_rows=1):
    return lax.broadcasted_iota(jnp.int32, (shape_rows, LANES), 1) < HEAD_DIM


def _proj_body(x_ref, g_ref, w_ref, gain_ref, *out_refs, groups, chunk):
    x = x_ref[...]
    ms = jnp.mean(x * x, axis=-1, keepdims=True)
    h = ((x * lax.rsqrt(ms + EPS)) * g_ref[...]).astype(BF16)
    lo = _lane_lo()
    for out_ref, (col0, width, kind, gain0) in zip(out_refs, groups):
        for c0 in range(0, width, chunk):
            cw = min(chunk, width - c0)
            acc = jnp.dot(h, w_ref[:, col0 + c0:col0 + c0 + cw], preferred_element_type=F32)
            if kind == "norm":
                for b in range(cw // LANES):
                    t = acc[:, b * LANES:(b + 1) * LANES]
                    sq = t * t
                    ss_lo = jnp.sum(jnp.where(lo, sq, 0.0), axis=-1, keepdims=True)
                    ss_hi = jnp.sum(jnp.where(lo, 0.0, sq), axis=-1, keepdims=True)
                    msq = jnp.where(lo, ss_lo, ss_hi) * (1.0 / HEAD_DIM)
                    g0 = gain0 + c0 + b * LANES
                    y = (t * lax.rsqrt(msq + EPS)) * gain_ref[:, g0:g0 + LANES]
                    out_ref[:, c0 + b * LANES:c0 + (b + 1) * LANES] = y.astype(out_ref.dtype)
            else:
                out_ref[:, c0:c0 + cw] = acc.astype(out_ref.dtype)


def _proj(x, g, w, gains, groups, out_dtypes, tm, name):
    t, d = x.shape
    n = w.shape[1]
    out_shape = [jax.ShapeDtypeStruct((t, gw[1]), dt) for gw, dt in zip(groups, out_dtypes)]
    out_specs = [pl.BlockSpec((tm, gw[1]), lambda i: (i, 0)) for gw in groups]
    return pl.pallas_call(
        functools.partial(_proj_body, groups=tuple(groups), chunk=512),
        grid=(t // tm,),
        in_specs=[
            pl.BlockSpec((tm, d), lambda i: (i, 0)),
            pl.BlockSpec((1, d), lambda i: (0, 0)),
            pl.BlockSpec((d, n), lambda i: (0, 0)),
            pl.BlockSpec((1, gains.shape[1]), lambda i: (0, 0)),
        ],
        out_specs=out_specs,
        out_shape=out_shape,
        compiler_params=pltpu.CompilerParams(
            dimension_semantics=("arbitrary",), vmem_limit_bytes=VMEM_LIMIT),
        name=name,
    )(x, g, w, gains)


def _swa_body(sink_ref, q_ref, kp_ref, kc_ref, vp_ref, vc_ref, o_ref, *, tq, slopes):
    i = pl.program_id(1)
    k = jnp.concatenate([kp_ref[...], kc_ref[...]], axis=0)
    v = jnp.concatenate([vp_ref[...], vc_ref[...]], axis=0)
    nk = tq + WINDOW
    row = lax.broadcasted_iota(jnp.int32, (tq, nk), 0)
    col = lax.broadcasted_iota(jnp.int32, (tq, nk), 1)
    dist = row - col + WINDOW
    allowed = (dist >= 0) & (dist < WINDOW) & ((col >= WINDOW) | (i > 0))
    distf = dist.astype(F32)
    lo = _lane_lo()
    for j in range(A_Q_HEADS // 2):
        qb = q_ref[:, j * LANES:(j + 1) * LANES]
        halves = []
        for hh in range(2):
            head = A_PERM[2 * j + hh]
            qm = jnp.where(lo if hh == 0 else jnp.logical_not(lo), qb, jnp.zeros_like(qb))
            s = lax.dot_general(qm, k, _NT, preferred_element_type=F32)
            s = jnp.where(allowed, s - slopes[head] * distf, NEG)
            sink = sink_ref[head]
            m = jnp.maximum(jnp.max(s, axis=-1, keepdims=True), sink)
            e = jnp.exp(s - m)
            den = jnp.sum(e, axis=-1, keepdims=True) + jnp.exp(sink - m)
            p = (e / den).astype(BF16)
            halves.append(jnp.dot(p, v, preferred_element_type=F32))
        o_ref[:, j * LANES:(j + 1) * LANES] = jnp.where(lo, halves[0], halves[1]).astype(o_ref.dtype)


def _swa(aq, ak, av, sinks, batch, seq, tq=256):
    t = aq.shape[0]
    nq = seq // tq
    r = tq // WINDOW
    nb = seq // WINDOW
    slopes = tuple(float(s) for s in _alibi_slopes()[:A_Q_HEADS])
    prev_map = lambda b, i: (jnp.maximum(b * nb + r * i - 1, b * nb), 0)
    cur_map = lambda b, i: (b * nq + i, 0)
    return pl.pallas_call(
        functools.partial(_swa_body, tq=tq, slopes=slopes),
        grid=(batch, nq),
        in_specs=[
            pl.BlockSpec(memory_space=pltpu.SMEM),
            pl.BlockSpec((tq, A_Q_HEADS * HEAD_DIM), cur_map),
            pl.BlockSpec((WINDOW, LANES), prev_map),
            pl.BlockSpec((tq, LANES), cur_map),
            pl.BlockSpec((WINDOW, LANES), prev_map),
            pl.BlockSpec((tq, LANES), cur_map),
        ],
        out_specs=pl.BlockSpec((tq, A_Q_HEADS * HEAD_DIM), cur_map),
        out_shape=jax.ShapeDtypeStruct((t, A_Q_HEADS * HEAD_DIM), BF16),
        compiler_params=pltpu.CompilerParams(
            dimension_semantics=("arbitrary", "arbitrary"), vmem_limit_bytes=VMEM_LIMIT),
        name="swa_attn",
    )(sinks, aq, ak, ak, av, av)


def _flash_scratch(tq, tk):
    return [
        pltpu.VMEM((2 * tq, 2 * LANES), BF16),
        pltpu.VMEM((2, 2 * tq, tk), F32),
        pltpu.VMEM((2, 2 * tq, tk), BF16),
        pltpu.VMEM((2, 2 * tq, LANES), F32),
        pltpu.VMEM((2 * tq, LANES), F32),
        pltpu.VMEM((2 * tq, 2 * LANES), F32),
    ]


def _flash_loop(scratch, k_ref, ks_ref, v_ref, o_ref, *, tq, tk, diag, n_past, diag_mask):
    qa_ref, s_ref, p_ref, al_ref, m_ref, acc_ref = scratch

    def stage_a(slot, c, mask=None):
        off = pl.multiple_of(c * tk, tk)
        ka = jnp.concatenate([k_ref[pl.ds(off, tk), :], ks_ref[pl.ds(off, tk), :]], axis=1)
        s = lax.dot_general(qa_ref[...], ka, _NT, preferred_element_type=F32)
        if mask is not None:
            s = jnp.where(mask, s, NEG)
        s_ref[slot] = s

    def stage_b(slot):
        s = s_ref[slot]
        m_prev = m_ref[...]
        m_next = jnp.maximum(m_prev, jnp.max(s, axis=-1, keepdims=True))
        al_ref[slot] = jnp.exp(m_prev - m_next)
        p_ref[slot] = jnp.exp(s - pltpu.repeat(m_next, tk // LANES, axis=1)).astype(BF16)
        m_ref[...] = m_next

    def stage_c(slot, c):
        off = pl.multiple_of(c * tk, tk)
        va = jnp.concatenate([v_ref[pl.ds(off, tk), :], jnp.ones((tk, LANES), BF16)], axis=1)
        pv = jnp.dot(p_ref[slot], va, preferred_element_type=F32)
        acc_ref[...] = pltpu.repeat(al_ref[slot], 2, axis=1) * acc_ref[...] + pv

    m_ref[...] = jnp.full(m_ref.shape, M_INIT, F32)
    acc_ref[...] = jnp.zeros(acc_ref.shape, F32)
    stage_a(0, diag, diag_mask)
    stage_b(0)
    stage_a(1, 0)

    def pair(u, carry):
        t = 2 * u + 1
        stage_c(0, jnp.where(u == 0, diag, t - 2))
        stage_b(1)
        stage_a(0, t)
        stage_c(1, t - 1)
        stage_b(0)
        stage_a(1, t + 1)
        return carry

    lax.fori_loop(0, n_past // 2, pair, 0)
    odd = (n_past % 2) == 1

    @pl.when(odd)
    def _():
        stage_c(0, jnp.where(n_past == 1, diag, n_past - 2))
        stage_b(1)
        stage_c(1, n_past - 1)

    @pl.when(jnp.logical_not(odd))
    def _():
        stage_c(0, jnp.where(n_past == 0, diag, n_past - 1))

    outs = [acc_ref[h * tq:(h + 1) * tq, :LANES] / acc_ref[h * tq:(h + 1) * tq, LANES:]
            for h in range(2)]
    o_ref[...] = jnp.where(_lane_lo(), outs[0], outs[1]).astype(o_ref.dtype)


FOX_SIDE_LANES = 6


def _fox_body(q_ref, qs_ref, k_ref, ks_ref, v_ref, o_ref, *scratch, tq):
    i = pl.program_id(2)
    qa_ref = scratch[0]
    lane = lax.broadcasted_iota(jnp.int32, (1, LANES), 1)
    q = q_ref[...]
    qs = qs_ref[...]
    for h in range(2):
        qm = jnp.where((lane < HEAD_DIM) == (h == 0), q, jnp.zeros_like(q))
        side = (lane >= h * FOX_SIDE_LANES) & (lane < (h + 1) * FOX_SIDE_LANES)
        qa_ref[h * tq:(h + 1) * tq, :] = jnp.concatenate(
            [qm, jnp.where(side, qs, jnp.zeros_like(qs))], axis=1)
    row = lax.broadcasted_iota(jnp.int32, (tq, tq), 0)
    col = lax.broadcasted_iota(jnp.int32, (tq, tq), 1)
    causal = col <= row
    _flash_loop(scratch, k_ref, ks_ref, v_ref, o_ref, tq=tq, tk=tq, diag=i, n_past=i,
                diag_mask=jnp.concatenate([causal, causal], axis=0))


def _fox(cq, qside, ck, kside, cv, batch, seq, tq=512):
    t, c = cq.shape
    nq = seq // tq
    q_map = lambda b, p, i: (b * nq + i, p)
    kv_map = lambda b, p, i: (b, p)
    return pl.pallas_call(
        functools.partial(_fox_body, tq=tq),
        grid=(batch, c // LANES, nq),
        in_specs=[
            pl.BlockSpec((tq, LANES), q_map),
            pl.BlockSpec((tq, LANES), q_map),
            pl.BlockSpec((seq, LANES), kv_map),
            pl.BlockSpec((seq, LANES), kv_map),
            pl.BlockSpec((seq, LANES), kv_map),
        ],
        out_specs=pl.BlockSpec((tq, LANES), q_map),
        out_shape=jax.ShapeDtypeStruct((t, c), BF16),
        scratch_shapes=_flash_scratch(tq, tq),
        compiler_params=pltpu.CompilerParams(
            dimension_semantics=("arbitrary", "arbitrary", "arbitrary"),
            vmem_limit_bytes=VMEM_LIMIT),
        name="fox_attn",
    )(cq, qside, ck, kside, cv)


def _fox_decay_body(cf_ref, bf_ref, tri_ref, pq_ref, pk_ref, cq_ref, ck_ref,
                    qs_ref, ks_ref, carry_ref):
    @pl.when(pl.program_id(1) == 0)
    def _():
        carry_ref[...] = jnp.zeros(carry_ref.shape, F32)

    x = cf_ref[...] + bf_ref[...]
    logf = -(jnp.maximum(-x, 0.0) + jnp.log1p(jnp.exp(-jnp.abs(x))))
    tri = tri_ref[...]
    c = carry_ref[...]
    for piece in _split3(logf):
        c = c + jnp.dot(tri, piece, preferred_element_type=F32)
    carry_ref[...] = c[c.shape[0] - 1:, :]
    qs = cq_ref[...]
    ks = ck_ref[...]
    for n, piece in enumerate(_split3(c)):
        qs = qs + jnp.dot(piece, pq_ref[n], preferred_element_type=F32)
        ks = ks + jnp.dot(piece, pk_ref[n], preferred_element_type=F32)
    qs_ref[...] = qs.astype(BF16)
    ks_ref[...] = ks.astype(BF16)


def _fox_side_tables():
    width = C_HEADS // 2 * LANES
    pq = np.zeros((3, LANES, width), np.float32)
    pk = np.zeros((3, LANES, width), np.float32)
    cq = np.zeros((1, width), np.float32)
    ck = np.zeros((1, width), np.float32)
    for head in range(C_HEADS):
        base = (head // 2) * LANES + (head % 2) * FOX_SIDE_LANES
        for n in range(3):
            pq[n, head, base + n] = 1.0
            ck[0, base + n] = 1.0
            cq[0, base + 3 + n] = 1.0
            pk[n, head, base + 3 + n] = -1.0
    return (jnp.asarray(pq, BF16), jnp.asarray(pk, BF16), jnp.asarray(cq), jnp.asarray(ck))


def _fox_decay(cf, b_f, batch, seq, rows=256):
    t = cf.shape[0]
    width = C_HEADS // 2 * LANES
    nr = seq // rows
    bf_row = jnp.zeros((1, LANES), F32).at[0, :C_HEADS].set(b_f)
    tri = jnp.asarray(np.tril(np.ones((rows, rows), np.float32)), BF16)
    pq, pk, cq, ck = _fox_side_tables()
    row_map = lambda b, r: (b * nr + r, 0)
    const2 = lambda b, r: (0, 0)
    const3 = lambda b, r: (0, 0, 0)
    return pl.pallas_call(
        _fox_decay_body,
        grid=(batch, nr),
        in_specs=[
            pl.BlockSpec((rows, LANES), row_map),
            pl.BlockSpec((1, LANES), const2),
            pl.BlockSpec((rows, rows), const2),
            pl.BlockSpec((3, LANES, width), const3),
            pl.BlockSpec((3, LANES, width), const3),
            pl.BlockSpec((1, width), const2),
            pl.BlockSpec((1, width), const2),
        ],
        out_specs=[pl.BlockSpec((rows, width), row_map), pl.BlockSpec((rows, width), row_map)],
        out_shape=[jax.ShapeDtypeStruct((t, width), BF16), jax.ShapeDtypeStruct((t, width), BF16)],
        scratch_shapes=[pltpu.VMEM((1, LANES), F32)],
        compiler_params=pltpu.CompilerParams(
            dimension_semantics=("arbitrary", "arbitrary"), vmem_limit_bytes=VMEM_LIMIT),
        name="fox_decay",
    )(cf, bf_row, tri, pq, pk, cq, ck)


MOBA_MAX_BLOCKS = 32
_L_POS, _L_BLK, _L_ONE = MOBA_MAX_BLOCKS, MOBA_MAX_BLOCKS + 3, MOBA_MAX_BLOCKS + 6


def _moba_ktab(seq):
    tab = np.zeros((seq, LANES), np.float32)
    s = np.arange(seq)
    tab[s, s // MOBA_BLOCK] = 1.0
    for n in range(3):
        tab[:, _L_POS + n] = s % MOBA_BLOCK
        tab[:, _L_BLK + n] = s // MOBA_BLOCK
        tab[:, _L_ONE + n] = 1.0
    return jnp.asarray(tab, BF16)


def _moba_body(slope_ref, q_ref, k_ref, kt_ref, v_ref, km_ref, o_ref, *scratch, tk):
    pair = pl.program_id(1)
    i = pl.program_id(2)
    tq = MOBA_BLOCK
    qa_ref = scratch[0]
    lane = lax.broadcasted_iota(jnp.int32, (1, LANES), 1)
    q = q_ref[...]
    km_pieces = _split3(km_ref[0])
    blk = lax.broadcasted_iota(jnp.int32, (MOBA_MAX_BLOCKS, tq), 0)
    blkf = blk.astype(F32)
    past = blk < i
    r16 = lax.broadcasted_iota(jnp.int32, (16, tq), 0)
    t_abs = (i * tq + lax.broadcasted_iota(jnp.int32, (1, tq), 1)).astype(F32)
    for h in range(2):
        qm = jnp.where((lane < HEAD_DIM) == (h == 0), q, jnp.zeros_like(q))
        gate = jnp.zeros((MOBA_MAX_BLOCKS, tq), F32)
        for piece in km_pieces:
            gate = gate + lax.dot_general(piece, qm, _NT, preferred_element_type=F32)
        g = jnp.where(past, gate, NEG)
        sel = jnp.zeros((MOBA_MAX_BLOCKS, tq), jnp.bool_)
        for _ in range(MOBA_TOPK):
            mx = jnp.max(g, axis=0, keepdims=True)
            first = jnp.min(jnp.where(g == mx, blkf, 1e9), axis=0, keepdims=True)
            pick = blkf == first
            sel = sel | pick
            g = jnp.where(pick, -jnp.inf, g)
        sel = (sel & past) | (blk == i)
        selbias = jnp.where(sel, 0.0, NEG)
        slope = jnp.full((1, tq), slope_ref[2 * pair + h], F32)
        rows = _split3(slope) + _split3(slope * MOBA_BLOCK) + _split3(-(slope * t_abs))
        extra = jnp.zeros((16, tq), F32)
        for n, rv in enumerate(rows):
            extra = jnp.where(r16 == n, rv.astype(F32), extra)
        side_t = jnp.concatenate(
            [selbias, extra, jnp.zeros((LANES - MOBA_MAX_BLOCKS - 16, tq), F32)], axis=0)
        qa_ref[h * tq:(h + 1) * tq, :] = jnp.concatenate([qm, side_t.T.astype(BF16)], axis=1)
    r = tk // tq
    n_main = i // r
    row = lax.broadcasted_iota(jnp.int32, (tq, tk), 0)
    col = lax.broadcasted_iota(jnp.int32, (tq, tk), 1)
    causal = (col - row) <= (i - n_main * r) * tq
    _flash_loop(scratch, k_ref, kt_ref, v_ref, o_ref, tq=tq, tk=tk, diag=n_main, n_past=n_main,
                diag_mask=jnp.concatenate([causal, causal], axis=0))


def _moba(bq, bk, bv, kmean, batch, seq, tk=512):
    t, c = bq.shape
    tq = MOBA_BLOCK
    nq = seq // tq
    assert nq <= MOBA_MAX_BLOCKS and seq % tk == 0
    slopes = jnp.asarray(_alibi_slopes()[A_Q_HEADS:])
    ktab = _moba_ktab(seq)
    q_map = lambda b, p, i: (b * nq + i, p)
    kv_map = lambda b, p, i: (b, p)
    return pl.pallas_call(
        functools.partial(_moba_body, tk=tk),
        grid=(batch, c // LANES, nq),
        in_specs=[
            pl.BlockSpec(memory_space=pltpu.SMEM),
            pl.BlockSpec((tq, LANES), q_map),
            pl.BlockSpec((seq, LANES), kv_map),
            pl.BlockSpec((seq, LANES), lambda b, p, i: (0, 0)),
            pl.BlockSpec((seq, LANES), kv_map),
            pl.BlockSpec((1, MOBA_MAX_BLOCKS, LANES), lambda b, p, i: (b, 0, p)),
        ],
        out_specs=pl.BlockSpec((tq, LANES), q_map),
        out_shape=jax.ShapeDtypeStruct((t, c), BF16),
        scratch_shapes=_flash_scratch(tq, tk),
        compiler_params=pltpu.CompilerParams(
            dimension_semantics=("arbitrary", "arbitrary", "arbitrary"),
            vmem_limit_bytes=VMEM_LIMIT),
        name="moba_attn",
    )(slopes, bq, bk, ktab, bv, kmean)


def _kmean_body(k_ref, o_ref, *, nblk):
    for j in range(nblk):
        kb = k_ref[j * MOBA_BLOCK:(j + 1) * MOBA_BLOCK, :].astype(F32)
        o_ref[0, j:j + 1, :] = jnp.mean(kb, axis=0, keepdims=True)
    if nblk < MOBA_MAX_BLOCKS:
        o_ref[0, nblk:, :] = jnp.zeros((MOBA_MAX_BLOCKS - nblk, o_ref.shape[2]), F32)


def _kmean(bk, batch, seq):
    c = bk.shape[1]
    nblk = seq // MOBA_BLOCK
    return pl.pallas_call(
        functools.partial(_kmean_body, nblk=nblk),
        grid=(batch,),
        in_specs=[pl.BlockSpec((seq, c), lambda b: (b, 0))],
        out_specs=pl.BlockSpec((1, MOBA_MAX_BLOCKS, c), lambda b: (b, 0, 0)),
        out_shape=jax.ShapeDtypeStruct((batch, MOBA_MAX_BLOCKS, c), F32),
        compiler_params=pltpu.CompilerParams(
            dimension_semantics=("arbitrary",), vmem_limit_bytes=VMEM_LIMIT),
        name="moba_kmean",
    )(bk)


def _out_body(*refs, n_parts):
    x_ref = refs[0]
    part_refs = refs[1:1 + n_parts]
    mq_ref, z_ref, mk_ref, mv_ref, w_ref, o_ref, y_ref = refs[1 + n_parts:]
    lo = _lane_lo()
    col = 0
    for pr in part_refs:
        w = pr.shape[1]
        zz = z_ref[:, col:col + w].astype(F32)
        y_ref[:, col:col + w] = (pr[...].astype(F32) * (zz / (1.0 + jnp.exp(-zz)))).astype(BF16)
        col += w
    for pp in range(MEM_HEADS // 2):
        sl = slice(pp * LANES, (pp + 1) * LANES)
        q = mq_ref[:, sl]
        k = mk_ref[:, sl]
        v = mv_ref[:, sl]
        halves = []
        for hh in range(2):
            qm = jnp.where(lo if hh == 0 else jnp.logical_not(lo), q, jnp.zeros_like(q))
            s = lax.dot_general(qm, k, _NT, preferred_element_type=F32)
            e = jnp.exp(s - jnp.max(s, axis=-1, keepdims=True))
            p = (e / jnp.sum(e, axis=-1, keepdims=True)).astype(BF16)
            halves.append(jnp.dot(p, v, preferred_element_type=F32))
        ym = jnp.where(lo, halves[0], halves[1])
        zz = z_ref[:, col + pp * LANES:col + (pp + 1) * LANES].astype(F32)
        y_ref[:, col + pp * LANES:col + (pp + 1) * LANES] = (ym * (zz / (1.0 + jnp.exp(-zz)))).astype(BF16)
    o_ref[...] = x_ref[...] + jnp.dot(y_ref[...], w_ref[...], preferred_element_type=F32)


def _out_proj(x, parts, mq, z, mk, mv, w_out, seq, tm=512):
    t, d = x.shape
    mem_len = mk.shape[0] // (t // seq)
    per_b = seq // tm
    row_map = lambda i: (i, 0)
    mem_map = lambda i: (i // per_b, 0)
    mix = z.shape[1]
    return pl.pallas_call(
        functools.partial(_out_body, n_parts=len(parts)),
        grid=(t // tm,),
        in_specs=[pl.BlockSpec((tm, d), row_map)]
        + [pl.BlockSpec((tm, p.shape[1]), row_map) for p in parts]
        + [
            pl.BlockSpec((tm, mq.shape[1]), row_map),
            pl.BlockSpec((tm, mix), row_map),
            pl.BlockSpec((mem_len, mk.shape[1]), mem_map),
            pl.BlockSpec((mem_len, mv.shape[1]), mem_map),
            pl.BlockSpec((mix, d), lambda i: (0, 0)),
        ],
        out_specs=pl.BlockSpec((tm, d), row_map),
        out_shape=jax.ShapeDtypeStruct((t, d), F32),
        scratch_shapes=[pltpu.VMEM((tm, mix), BF16)],
        compiler_params=pltpu.CompilerParams(
            dimension_semantics=("arbitrary",), vmem_limit_bytes=VMEM_LIMIT),
        name="out_proj",
    )(x, *parts, mq, z, mk, mv, w_out)


def _head_gain(g, heads, scale=1.0):
    return jnp.tile(g.astype(F32), heads) * scale


def _memory_kv(mem2, mem_norm_g, w_mem_kv, k_gain):
    d = mem2.shape[1]
    w = MEM_HEADS * HEAD_DIM
    gains = _head_gain(k_gain, MEM_HEADS)[None, :]
    mk, mv = _proj(mem2, mem_norm_g[None, :].astype(F32), w_mem_kv.astype(BF16), gains,
                   [(0, w, "norm", 0), (w, w, "plain", 0)], [BF16, BF16], tm=256, name="mem_kv")
    return mk, mv


def _perm_heads(w, perm, axis):
    shp = w.shape
    n = shp[axis] // HEAD_DIM
    w = w.reshape(shp[:axis] + (n, HEAD_DIM) + shp[axis + 1:])
    w = jnp.take(w, jnp.asarray(perm), axis=axis)
    return w.reshape(shp)


def _even_layer(x2, mem2, batch, seq, norm_g, w_in, qk_g, sinks, mem_norm_g, w_mem_kv, w_out):
    aw, kvw, bw, mw = A_Q_HEADS * HEAD_DIM, A_KV_HEADS * HEAD_DIM, B_HEADS * HEAD_DIM, MEM_HEADS * HEAD_DIM
    mix = aw + bw + mw
    o = np.cumsum([0, aw, kvw, kvw, bw, bw, bw, mw, mix])
    w = jnp.concatenate([
        _perm_heads(w_in[:, o[0]:o[1]], A_PERM, 1),
        w_in[:, o[1]:o[7]],
        _perm_heads(w_in[:, o[7]:o[7] + aw], A_PERM, 1),
        w_in[:, o[7] + aw:],
    ], axis=1).astype(BF16)
    gains = jnp.concatenate([
        _head_gain(qk_g[0], A_Q_HEADS, SCALE), _head_gain(qk_g[1], A_KV_HEADS),
        _head_gain(qk_g[2], B_HEADS, SCALE), _head_gain(qk_g[3], B_HEADS),
        _head_gain(qk_g[4], MEM_HEADS, SCALE)])[None, :]
    go = np.cumsum([0, aw, kvw, bw, bw])
    groups = [
        (int(o[0]), aw, "norm", int(go[0])), (int(o[1]), kvw, "norm", int(go[1])),
        (int(o[2]), kvw, "plain", 0),
        (int(o[3]), bw, "norm", int(go[2])), (int(o[4]), bw, "norm", int(go[3])),
        (int(o[5]), bw, "plain", 0),
        (int(o[6]), mw, "norm", int(go[4])), (int(o[7]), mix, "plain", 0)]
    aq, ak, av, bq, bk, bv, mq, z = _proj(
        x2, norm_g[None, :].astype(F32), w, gains, groups, [BF16] * 8, tm=512, name="even_in_proj")
    mk, mv = _memory_kv(mem2, mem_norm_g, w_mem_kv, qk_g[5])
    ya = _swa(aq, ak, av, sinks.astype(F32), batch, seq)
    yb = _moba(bq, bk, bv, _kmean(bk, batch, seq), batch, seq)
    w_o = jnp.concatenate([_perm_heads(w_out[:aw], A_PERM, 0), w_out[aw:]], axis=0).astype(BF16)
    return _out_proj(x2, [ya, yb], mq, z, mk, mv, w_o, seq)


def _odd_layer(x2, mem2, batch, seq, norm_g, w_in, qk_g, b_f, mem_norm_g, w_mem_kv, w_out):
    cw, mw = C_HEADS * HEAD_DIM, MEM_HEADS * HEAD_DIM
    mix = cw + mw
    o = np.cumsum([0, cw, cw, cw, C_HEADS, mw, mix])
    d = w_in.shape[0]
    w = jnp.concatenate([
        w_in[:, o[0]:o[4]], jnp.zeros((d, LANES - C_HEADS), w_in.dtype), w_in[:, o[4]:]],
        axis=1).astype(BF16)
    po = np.cumsum([0, cw, cw, cw, LANES, mw, mix])
    gains = jnp.concatenate([
        _head_gain(qk_g[0], C_HEADS, SCALE), _head_gain(qk_g[1], C_HEADS),
        _head_gain(qk_g[2], MEM_HEADS, SCALE)])[None, :]
    groups = [
        (int(po[0]), cw, "norm", 0), (int(po[1]), cw, "norm", cw), (int(po[2]), cw, "plain", 0),
        (int(po[3]), LANES, "plain", 0), (int(po[4]), mw, "norm", 2 * cw), (int(po[5]), mix, "plain", 0)]
    cq, ck, cv, cf, mq, z = _proj(
        x2, norm_g[None, :].astype(F32), w, gains, groups, [BF16, BF16, BF16, F32, BF16, BF16],
        tm=512, name="odd_in_proj")
    mk, mv = _memory_kv(mem2, mem_norm_g, w_mem_kv, qk_g[3])
    qside, kside = _fox_decay(cf, b_f.astype(F32), batch, seq)
    yc = _fox(cq, qside, ck, kside, cv, batch, seq)
    return _out_proj(x2, [yc], mq, z, mk, mv, w_out.astype(BF16), seq)


def kernel(x, mem, e_norm, e_w_in, e_qk_norm, e_sinks, e_mem_norm, e_w_mem_kv, e_w_out,
           o_norm, o_w_in, o_qk_norm, o_b_f, o_mem_norm, o_w_mem_kv, o_w_out):
    batch, seq, d = x.shape
    x2 = x.reshape(batch * seq, d)
    mem2 = mem.reshape(batch * mem.shape[1], d)
    depth = e_norm.shape[0] + o_norm.shape[0]
    for layer in range(depth):
        i = layer // 2
        if layer % 2 == 0:
            x2 = _even_layer(x2, mem2, batch, seq, e_norm[i], e_w_in[i], e_qk_norm[i], e_sinks[i],
                             e_mem_norm[i], e_w_mem_kv[i], e_w_out[i])
        else:
            x2 = _odd_layer(x2, mem2, batch, seq, o_norm[i], o_w_in[i], o_qk_norm[i], o_b_f[i],
                            o_mem_norm[i], o_w_mem_kv[i], o_w_out[i])
    return x2.reshape(batch, seq, d)
```

```python
import functools

import numpy as np
import jax
import jax.numpy as jnp
from jax import lax
from jax.experimental import pallas as pl
from jax.experimental.pallas import tpu as pltpu

HEAD_DIM = 64
LANES = 128
A_Q_HEADS, A_KV_HEADS, WINDOW = 6, 2, 128
B_HEADS, MOBA_BLOCK, MOBA_TOPK = 6, 256, 3
C_HEADS = 12
MEM_HEADS = 4
N_ALIBI = A_Q_HEADS + B_HEADS
EPS = 1e-6
NEG = -1e30
M_INIT = -1e29
SCALE = HEAD_DIM ** -0.5
LOG2E = 1.4426950408889634
VMEM_LIMIT = 56 * 1024 * 1024

A_PERM = (0, 3, 2, 5, 1, 4)

F32 = jnp.float32
BF16 = jnp.bfloat16

_NT = (((1,), (1,)), ((), ()))


def _alibi_slopes():
    h = np.arange(1, N_ALIBI + 1, dtype=np.float32)
    return (np.float32(2.0) ** (np.float32(-8.0) * h / np.float32(N_ALIBI))).astype(np.float32)


def _split3(x):
    hi = x.astype(BF16)
    r1 = x - hi.astype(F32)
    mid = r1.astype(BF16)
    lo = (r1 - mid.astype(F32)).astype(BF16)
    return hi, mid, lo


def _lane_lo(shape_rows=1):
    return lax.broadcasted_iota(jnp.int32, (shape_rows, LANES), 1) < HEAD_DIM


def _proj_body(x_ref, g_ref, w_ref, gain_ref, *out_refs, groups, chunk):
    x = x_ref[...]
    ms = jnp.mean(x * x, axis=-1, keepdims=True)
    h = ((x * lax.rsqrt(ms + EPS)) * g_ref[...]).astype(BF16)
    lo = _lane_lo()
    for out_ref, (col0, width, kind, gain0) in zip(out_refs, groups):
        for c0 in range(0, width, chunk):
            cw = min(chunk, width - c0)
            acc = jnp.dot(h, w_ref[:, col0 + c0:col0 + c0 + cw], preferred_element_type=F32)
            if kind == "norm":
                for b in range(cw // LANES):
                    t = acc[:, b * LANES:(b + 1) * LANES]
                    sq = t * t
                    ss_lo = jnp.sum(jnp.where(lo, sq, 0.0), axis=-1, keepdims=True)
                    ss_hi = jnp.sum(jnp.where(lo, 0.0, sq), axis=-1, keepdims=True)
                    msq = jnp.where(lo, ss_lo, ss_hi) * (1.0 / HEAD_DIM)
                    g0 = gain0 + c0 + b * LANES
                    y = (t * lax.rsqrt(msq + EPS)) * gain_ref[:, g0:g0 + LANES]
                    out_ref[:, c0 + b * LANES:c0 + (b + 1) * LANES] = y.astype(out_ref.dtype)
            else:
                out_ref[:, c0:c0 + cw] = acc.astype(out_ref.dtype)


def _proj(x, g, w, gains, groups, out_dtypes, tm, name):
    t, d = x.shape
    n = w.shape[1]
    out_shape = [jax.ShapeDtypeStruct((t, gw[1]), dt) for gw, dt in zip(groups, out_dtypes)]
    out_specs = [pl.BlockSpec((tm, gw[1]), lambda i: (i, 0)) for gw in groups]
    return pl.pallas_call(
        functools.partial(_proj_body, groups=tuple(groups), chunk=512),
        grid=(t // tm,),
        in_specs=[
            pl.BlockSpec((tm, d), lambda i: (i, 0)),
            pl.BlockSpec((1, d), lambda i: (0, 0)),
            pl.BlockSpec((d, n), lambda i: (0, 0)),
            pl.BlockSpec((1, gains.shape[1]), lambda i: (0, 0)),
        ],
        out_specs=out_specs,
        out_shape=out_shape,
        compiler_params=pltpu.CompilerParams(
            dimension_semantics=("arbitrary",), vmem_limit_bytes=VMEM_LIMIT),
        name=name,
    )(x, g, w, gains)


SWA_BLOCKS_PER_STEP = 4


def _swa_side_tables():
    slopes = _alibi_slopes()[:A_Q_HEADS].astype(np.float64) * LOG2E
    r = np.arange(WINDOW, dtype=np.float64)
    qside = np.zeros((A_Q_HEADS, WINDOW, LANES), np.float32)
    for pos, head in enumerate(A_PERM):
        qside[pos, :, 0] = slopes[head]
        qside[pos, :, 3] = -slopes[head] * (r + WINDOW)
    qside = jnp.asarray(qside)
    pieces = jnp.zeros(qside.shape, BF16)
    for base in (0, 3):
        for n, piece in enumerate(_split3(qside[:, :, base])):
            pieces = pieces.at[:, :, base + n].set(piece)
    kside = np.zeros((2 * WINDOW, LANES), np.float32)
    kside[:, 0:3] = np.arange(2 * WINDOW, dtype=np.float32)[:, None]
    kside[:, 3:6] = 1.0
    return pieces.reshape(A_Q_HEADS * WINDOW, LANES), jnp.asarray(kside, BF16)


def _swa_body(sink_ref, q_ref, kp_ref, kc_ref, vp_ref, vc_ref, qs_ref, ks_ref, o_ref):
    i = pl.program_id(1)
    w = WINDOW
    k = jnp.concatenate([kp_ref[...], kc_ref[...]], axis=0)
    v = jnp.concatenate([vp_ref[...], vc_ref[...]], axis=0)
    lo = _lane_lo()
    r_i = lax.broadcasted_iota(jnp.int32, (w, 2 * w), 0)
    c_i = lax.broadcasted_iota(jnp.int32, (w, 2 * w), 1)
    band = (c_i <= r_i + w) & (c_i > r_i)
    band_first = band & ((c_i >= w) | (i > 0))
    sink = jnp.concatenate(
        [jnp.full((w, LANES), sink_ref[head] * LOG2E, F32) for head in A_PERM], axis=0)
    ones = jnp.ones((2 * w, LANES), BF16)
    for n in range(SWA_BLOCKS_PER_STEP):
        qb = q_ref[n * w:(n + 1) * w, :]
        parts = []
        for pos in range(A_Q_HEADS):
            blk = qb[:, (pos // 2) * LANES:(pos // 2 + 1) * LANES]
            parts.append(jnp.where(lo if pos % 2 == 0 else jnp.logical_not(lo), blk,
                                   jnp.zeros_like(blk)))
        qa = jnp.concatenate([jnp.concatenate(parts, axis=0), qs_ref[...]], axis=1)
        ka = jnp.concatenate([k[n * w:(n + 2) * w, :], ks_ref[...]], axis=1)
        va = jnp.concatenate([v[n * w:(n + 2) * w, :], ones], axis=1)
        s = lax.dot_general(qa, ka, _NT, preferred_element_type=F32)
        mask = band_first if n == 0 else band
        s = jnp.where(jnp.concatenate([mask] * A_Q_HEADS, axis=0), s, NEG)
        m = jnp.maximum(jnp.max(s, axis=-1, keepdims=True), sink)
        p = jnp.exp2(s - _lane_tile(m, 2 * w // LANES)).astype(BF16)
        pv = jnp.dot(p, va, preferred_element_type=F32)
        out = pv[:, :LANES] / (pv[:, LANES:] + jnp.exp2(sink - m))
        for j in range(A_Q_HEADS // 2):
            o_ref[n * w:(n + 1) * w, j * LANES:(j + 1) * LANES] = jnp.where(
                lo, out[2 * j * w:(2 * j + 1) * w], out[(2 * j + 1) * w:(2 * j + 2) * w]
            ).astype(o_ref.dtype)


def _swa(aq, ak, av, sinks, batch, seq):
    t = aq.shape[0]
    tq = SWA_BLOCKS_PER_STEP * WINDOW
    nq = seq // tq
    nb = seq // WINDOW
    qside, kside = _swa_side_tables()
    prev_map = lambda b, i: (jnp.maximum(b * nb + SWA_BLOCKS_PER_STEP * i - 1, b * nb), 0)
    cur_map = lambda b, i: (b * nq + i, 0)
    const = lambda b, i: (0, 0)
    return pl.pallas_call(
        _swa_body,
        grid=(batch, nq),
        in_specs=[
            pl.BlockSpec(memory_space=pltpu.SMEM),
            pl.BlockSpec((tq, A_Q_HEADS * HEAD_DIM), cur_map),
            pl.BlockSpec((WINDOW, LANES), prev_map),
            pl.BlockSpec((tq, LANES), cur_map),
            pl.BlockSpec((WINDOW, LANES), prev_map),
            pl.BlockSpec((tq, LANES), cur_map),
            pl.BlockSpec(qside.shape, const),
            pl.BlockSpec(kside.shape, const),
        ],
        out_specs=pl.BlockSpec((tq, A_Q_HEADS * HEAD_DIM), cur_map),
        out_shape=jax.ShapeDtypeStruct((t, A_Q_HEADS * HEAD_DIM), BF16),
        compiler_params=pltpu.CompilerParams(
            dimension_semantics=("arbitrary", "arbitrary"), vmem_limit_bytes=VMEM_LIMIT),
        name="swa_attn",
    )(sinks, aq, ak, ak, av, av, qside, kside)


def _flash_scratch(tq, tk):
    return [
        pltpu.VMEM((2 * tq, 2 * LANES), BF16),
        pltpu.VMEM((2, 2 * tq, tk), F32),
        pltpu.VMEM((2, 2 * tq, tk), BF16),
        pltpu.VMEM((2, 2 * tq, LANES), F32),
        pltpu.VMEM((2 * tq, LANES), F32),
        pltpu.VMEM((2 * tq, 2 * LANES), F32),
    ]


def _lane_tile(x, n):
    return jnp.concatenate([x] * n, axis=1)


def _flash_loop(scratch, k_ref, ks_ref, v_ref, o_ref, *, tq, tk, diag, n_past, diag_mask):
    qa_ref, s_ref, p_ref, al_ref, m_ref, acc_ref = scratch

    def stage_a(slot, c, mask=None):
        off = pl.multiple_of(c * tk, tk)
        ka = jnp.concatenate([k_ref[pl.ds(off, tk), :], ks_ref[pl.ds(off, tk), :]], axis=1)
        s = lax.dot_general(qa_ref[...], ka, _NT, preferred_element_type=F32)
        if mask is not None:
            s = jnp.where(mask, s, NEG)
        s_ref[slot] = s

    def stage_b(slot):
        s = s_ref[slot]
        m_prev = m_ref[...]
        m_next = jnp.maximum(m_prev, jnp.max(s, axis=-1, keepdims=True))
        al_ref[slot] = jnp.exp2(m_prev - m_next)
        p_ref[slot] = jnp.exp2(s - _lane_tile(m_next, tk // LANES)).astype(BF16)
        m_ref[...] = m_next

    def stage_c(slot, c):
        off = pl.multiple_of(c * tk, tk)
        va = jnp.concatenate([v_ref[pl.ds(off, tk), :], jnp.ones((tk, LANES), BF16)], axis=1)
        pv = jnp.dot(p_ref[slot], va, preferred_element_type=F32)
        acc_ref[...] = _lane_tile(al_ref[slot], 2) * acc_ref[...] + pv

    m_ref[...] = jnp.full(m_ref.shape, M_INIT, F32)
    acc_ref[...] = jnp.zeros(acc_ref.shape, F32)
    stage_a(0, diag, diag_mask)
    stage_b(0)
    stage_a(1, 0)

    def pair(u, carry):
        t = 2 * u + 1
        stage_c(0, jnp.where(u == 0, diag, t - 2))
        stage_b(1)
        stage_a(0, t)
        stage_c(1, t - 1)
        stage_b(0)
        stage_a(1, t + 1)
        return carry

    lax.fori_loop(0, n_past // 2, pair, 0)
    odd = (n_past % 2) == 1

    @pl.when(odd)
    def _():
        stage_c(0, jnp.where(n_past == 1, diag, n_past - 2))
        stage_b(1)
        stage_c(1, n_past - 1)

    @pl.when(jnp.logical_not(odd))
    def _():
        stage_c(0, jnp.where(n_past == 0, diag, n_past - 1))

    outs = [acc_ref[h * tq:(h + 1) * tq, :LANES] / acc_ref[h * tq:(h + 1) * tq, LANES:]
            for h in range(2)]
    o_ref[...] = jnp.where(_lane_lo(), outs[0], outs[1]).astype(o_ref.dtype)


FOX_SIDE_LANES = 6


def _fox_body(q_ref, qs_ref, k_ref, ks_ref, v_ref, o_ref, *scratch, tq):
    i = pl.program_id(2)
    qa_ref = scratch[0]
    lane = lax.broadcasted_iota(jnp.int32, (1, LANES), 1)
    q = q_ref[...]
    qs = qs_ref[...]
    for h in range(2):
        qm = jnp.where((lane < HEAD_DIM) == (h == 0), q, jnp.zeros_like(q))
        side0 = (2 * pl.program_id(1) + h) * FOX_SIDE_LANES
        side = (lane >= side0) & (lane < side0 + FOX_SIDE_LANES)
        qa_ref[h * tq:(h + 1) * tq, :] = jnp.concatenate(
            [qm, jnp.where(side, qs, jnp.zeros_like(qs))], axis=1)
    row = lax.broadcasted_iota(jnp.int32, (tq, tq), 0)
    col = lax.broadcasted_iota(jnp.int32, (tq, tq), 1)
    causal = col <= row
    _flash_loop(scratch, k_ref, ks_ref, v_ref, o_ref, tq=tq, tk=tq, diag=i, n_past=i,
                diag_mask=jnp.concatenate([causal, causal], axis=0))


def _fox(cq, qside, ck, kside, cv, batch, seq, tq=512):
    t, c = cq.shape
    nq = seq // tq
    q_map = lambda b, p, i: (b * nq + i, p)
    kv_map = lambda b, p, i: (b, p)
    return pl.pallas_call(
        functools.partial(_fox_body, tq=tq),
        grid=(batch, c // LANES, nq),
        in_specs=[
            pl.BlockSpec((tq, LANES), q_map),
            pl.BlockSpec((tq, LANES), lambda b, p, i: (b * nq + i, 0)),
            pl.BlockSpec((seq, LANES), kv_map),
            pl.BlockSpec((seq, LANES), lambda b, p, i: (b, 0)),
            pl.BlockSpec((seq, LANES), kv_map),
        ],
        out_specs=pl.BlockSpec((tq, LANES), q_map),
        out_shape=jax.ShapeDtypeStruct((t, c), BF16),
        scratch_shapes=_flash_scratch(tq, tq),
        compiler_params=pltpu.CompilerParams(
            dimension_semantics=("arbitrary", "arbitrary", "arbitrary"),
            vmem_limit_bytes=VMEM_LIMIT),
        name="fox_attn",
    )(cq, qside, ck, kside, cv)


def _fox_decay_body(cf_ref, bf_ref, tri_ref, pq_ref, pk_ref, cq_ref, ck_ref,
                    qs_ref, ks_ref, carry_ref):
    @pl.when(pl.program_id(1) == 0)
    def _():
        carry_ref[...] = jnp.zeros(carry_ref.shape, F32)

    x = cf_ref[...] + bf_ref[...]
    logf = -(jnp.maximum(-x, 0.0) + jnp.log1p(jnp.exp(-jnp.abs(x))))
    tri = tri_ref[...]
    c = carry_ref[...]
    for piece in _split3(logf):
        c = c + jnp.dot(tri, piece, preferred_element_type=F32)
    carry_ref[...] = c[c.shape[0] - 1:, :]
    qs = cq_ref[...]
    ks = ck_ref[...]
    for n, piece in enumerate(_split3(c * LOG2E)):
        qs = qs + jnp.dot(piece, pq_ref[n], preferred_element_type=F32)
        ks = ks + jnp.dot(piece, pk_ref[n], preferred_element_type=F32)
    qs_ref[...] = qs.astype(BF16)
    ks_ref[...] = ks.astype(BF16)


def _fox_side_tables():
    width = LANES
    assert C_HEADS * FOX_SIDE_LANES <= LANES
    pq = np.zeros((3, LANES, width), np.float32)
    pk = np.zeros((3, LANES, width), np.float32)
    cq = np.zeros((1, width), np.float32)
    ck = np.zeros((1, width), np.float32)
    for head in range(C_HEADS):
        base = head * FOX_SIDE_LANES
        for n in range(3):
            pq[n, head, base + n] = 1.0
            ck[0, base + n] = 1.0
            cq[0, base + 3 + n] = 1.0
            pk[n, head, base + 3 + n] = -1.0
    return (jnp.asarray(pq, BF16), jnp.asarray(pk, BF16), jnp.asarray(cq), jnp.asarray(ck))


def _fox_decay(cf, b_f, batch, seq, rows=256):
    t = cf.shape[0]
    width = LANES
    nr = seq // rows
    bf_row = jnp.zeros((1, LANES), F32).at[0, :C_HEADS].set(b_f)
    tri = jnp.asarray(np.tril(np.ones((rows, rows), np.float32)), BF16)
    pq, pk, cq, ck = _fox_side_tables()
    row_map = lambda b, r: (b * nr + r, 0)
    const2 = lambda b, r: (0, 0)
    const3 = lambda b, r: (0, 0, 0)
    return pl.pallas_call(
        _fox_decay_body,
        grid=(batch, nr),
        in_specs=[
            pl.BlockSpec((rows, LANES), row_map),
            pl.BlockSpec((1, LANES), const2),
            pl.BlockSpec((rows, rows), const2),
            pl.BlockSpec((3, LANES, width), const3),
            pl.BlockSpec((3, LANES, width), const3),
            pl.BlockSpec((1, width), const2),
            pl.BlockSpec((1, width), const2),
        ],
        out_specs=[pl.BlockSpec((rows, width), row_map), pl.BlockSpec((rows, width), row_map)],
        out_shape=[jax.ShapeDtypeStruct((t, width), BF16), jax.ShapeDtypeStruct((t, width), BF16)],
        scratch_shapes=[pltpu.VMEM((1, LANES), F32)],
        compiler_params=pltpu.CompilerParams(
            dimension_semantics=("arbitrary", "arbitrary"), vmem_limit_bytes=VMEM_LIMIT),
        name="fox_decay",
    )(cf, bf_row, tri, pq, pk, cq, ck)


MOBA_MAX_BLOCKS = 32
_L_POS, _L_BLK, _L_ONE = MOBA_MAX_BLOCKS, MOBA_MAX_BLOCKS + 3, MOBA_MAX_BLOCKS + 6


def _moba_ktab(seq):
    tab = np.zeros((seq, LANES), np.float32)
    s = np.arange(seq)
    tab[s, s // MOBA_BLOCK] = 1.0
    for n in range(3):
        tab[:, _L_POS + n] = s % MOBA_BLOCK
        tab[:, _L_BLK + n] = s // MOBA_BLOCK
        tab[:, _L_ONE + n] = 1.0
    return jnp.asarray(tab, BF16)


def _moba_body(slope_ref, q_ref, k_ref, kt_ref, v_ref, km_ref, o_ref, *scratch, tk):
    pair = pl.program_id(1)
    i = pl.program_id(2)
    tq = MOBA_BLOCK
    qa_ref = scratch[0]
    lane = lax.broadcasted_iota(jnp.int32, (1, LANES), 1)
    q = q_ref[...]
    km_pieces = _split3(km_ref[0])
    blk = lax.broadcasted_iota(jnp.int32, (MOBA_MAX_BLOCKS, tq), 0)
    blkf = blk.astype(F32)
    past = blk < i
    r16 = lax.broadcasted_iota(jnp.int32, (16, tq), 0)
    t_abs = (i * tq + lax.broadcasted_iota(jnp.int32, (1, tq), 1)).astype(F32)
    for h in range(2):
        qm = jnp.where((lane < HEAD_DIM) == (h == 0), q, jnp.zeros_like(q))
        gate = jnp.zeros((MOBA_MAX_BLOCKS, tq), F32)
        for piece in km_pieces:
            gate = gate + lax.dot_general(piece, qm, _NT, preferred_element_type=F32)
        g = jnp.where(past, gate, NEG)
        sel = jnp.zeros((MOBA_MAX_BLOCKS, tq), jnp.bool_)
        for _ in range(MOBA_TOPK):
            mx = jnp.max(g, axis=0, keepdims=True)
            first = jnp.min(jnp.where(g == mx, blkf, 1e9), axis=0, keepdims=True)
            pick = blkf == first
            sel = sel | pick
            g = jnp.where(pick, -jnp.inf, g)
        sel = (sel & past) | (blk == i)
        selbias = jnp.where(sel, 0.0, NEG)
        slope = jnp.full((1, tq), slope_ref[2 * pair + h], F32)
        rows = _split3(slope) + _split3(slope * MOBA_BLOCK) + _split3(-(slope * t_abs))
        extra = jnp.zeros((16, tq), F32)
        for n, rv in enumerate(rows):
            extra = jnp.where(r16 == n, rv.astype(F32), extra)
        side_t = jnp.concatenate(
            [selbias, extra, jnp.zeros((LANES - MOBA_MAX_BLOCKS - 16, tq), F32)], axis=0)
        qa_ref[h * tq:(h + 1) * tq, :] = jnp.concatenate([qm, side_t.T.astype(BF16)], axis=1)
    r = tk // tq
    n_main = i // r
    row = lax.broadcasted_iota(jnp.int32, (tq, tk), 0)
    col = lax.broadcasted_iota(jnp.int32, (tq, tk), 1)
    causal = (col - row) <= (i - n_main * r) * tq
    _flash_loop(scratch, k_ref, kt_ref, v_ref, o_ref, tq=tq, tk=tk, diag=n_main, n_past=n_main,
                diag_mask=jnp.concatenate([causal, causal], axis=0))


def _moba(bq, bk, bv, kmean, batch, seq, tk=512):
    t, c = bq.shape
    tq = MOBA_BLOCK
    nq = seq // tq
    assert nq <= MOBA_MAX_BLOCKS and seq % tk == 0
    slopes = jnp.asarray(_alibi_slopes()[A_Q_HEADS:] * np.float32(LOG2E))
    ktab = _moba_ktab(seq)
    q_map = lambda b, p, i: (b * nq + i, p)
    kv_map = lambda b, p, i: (b, p)
    return pl.pallas_call(
        functools.partial(_moba_body, tk=tk),
        grid=(batch, c // LANES, nq),
        in_specs=[
            pl.BlockSpec(memory_space=pltpu.SMEM),
            pl.BlockSpec((tq, LANES), q_map),
            pl.BlockSpec((seq, LANES), kv_map),
            pl.BlockSpec((seq, LANES), lambda b, p, i: (0, 0)),
            pl.BlockSpec((seq, LANES), kv_map),
            pl.BlockSpec((1, MOBA_MAX_BLOCKS, LANES), lambda b, p, i: (b, 0, p)),
        ],
        out_specs=pl.BlockSpec((tq, LANES), q_map),
        out_shape=jax.ShapeDtypeStruct((t, c), BF16),
        scratch_shapes=_flash_scratch(tq, tk),
        compiler_params=pltpu.CompilerParams(
            dimension_semantics=("arbitrary", "arbitrary", "arbitrary"),
            vmem_limit_bytes=VMEM_LIMIT),
        name="moba_attn",
    )(slopes, bq, bk, ktab, bv, kmean)


def _kmean_body(k_ref, o_ref, *, nblk):
    for j in range(nblk):
        kb = k_ref[j * MOBA_BLOCK:(j + 1) * MOBA_BLOCK, :].astype(F32)
        o_ref[0, j:j + 1, :] = jnp.mean(kb, axis=0, keepdims=True)
    if nblk < MOBA_MAX_BLOCKS:
        o_ref[0, nblk:, :] = jnp.zeros((MOBA_MAX_BLOCKS - nblk, o_ref.shape[2]), F32)


def _kmean(bk, batch, seq):
    c = bk.shape[1]
    nblk = seq // MOBA_BLOCK
    return pl.pallas_call(
        functools.partial(_kmean_body, nblk=nblk),
        grid=(batch,),
        in_specs=[pl.BlockSpec((seq, c), lambda b: (b, 0))],
        out_specs=pl.BlockSpec((1, MOBA_MAX_BLOCKS, c), lambda b: (b, 0, 0)),
        out_shape=jax.ShapeDtypeStruct((batch, MOBA_MAX_BLOCKS, c), F32),
        compiler_params=pltpu.CompilerParams(
            dimension_semantics=("arbitrary",), vmem_limit_bytes=VMEM_LIMIT),
        name="moba_kmean",
    )(bk)


def _out_body(*refs, n_parts):
    x_ref = refs[0]
    part_refs = refs[1:1 + n_parts]
    mq_ref, z_ref, mk_ref, mv_ref, w_ref, o_ref, y_ref = refs[1 + n_parts:]
    lo = _lane_lo()
    col = 0
    for pr in part_refs:
        w = pr.shape[1]
        zz = z_ref[:, col:col + w].astype(F32)
        y_ref[:, col:col + w] = (pr[...].astype(F32) * (zz / (1.0 + jnp.exp(-zz)))).astype(BF16)
        col += w
    for pp in range(MEM_HEADS // 2):
        sl = slice(pp * LANES, (pp + 1) * LANES)
        q = mq_ref[:, sl]
        k = mk_ref[:, sl]
        v = mv_ref[:, sl]
        halves = []
        for hh in range(2):
            qm = jnp.where(lo if hh == 0 else jnp.logical_not(lo), q, jnp.zeros_like(q))
            s = lax.dot_general(qm, k, _NT, preferred_element_type=F32)
            e = jnp.exp(s - jnp.max(s, axis=-1, keepdims=True))
            p = (e / jnp.sum(e, axis=-1, keepdims=True)).astype(BF16)
            halves.append(jnp.dot(p, v, preferred_element_type=F32))
        ym = jnp.where(lo, halves[0], halves[1])
        zz = z_ref[:, col + pp * LANES:col + (pp + 1) * LANES].astype(F32)
        y_ref[:, col + pp * LANES:col + (pp + 1) * LANES] = (ym * (zz / (1.0 + jnp.exp(-zz)))).astype(BF16)
    o_ref[...] = x_ref[...] + jnp.dot(y_ref[...], w_ref[...], preferred_element_type=F32)


def _out_proj(x, parts, mq, z, mk, mv, w_out, seq, tm=512):
    t, d = x.shape
    mem_len = mk.shape[0] // (t // seq)
    per_b = seq // tm
    row_map = lambda i: (i, 0)
    mem_map = lambda i: (i // per_b, 0)
    mix = z.shape[1]
    return pl.pallas_call(
        functools.partial(_out_body, n_parts=len(parts)),
        grid=(t // tm,),
        in_specs=[pl.BlockSpec((tm, d), row_map)]
        + [pl.BlockSpec((tm, p.shape[1]), row_map) for p in parts]
        + [
            pl.BlockSpec((tm, mq.shape[1]), row_map),
            pl.BlockSpec((tm, mix), row_map),
            pl.BlockSpec((mem_len, mk.shape[1]), mem_map),
            pl.BlockSpec((mem_len, mv.shape[1]), mem_map),
            pl.BlockSpec((mix, d), lambda i: (0, 0)),
        ],
        out_specs=pl.BlockSpec((tm, d), row_map),
        out_shape=jax.ShapeDtypeStruct((t, d), F32),
        scratch_shapes=[pltpu.VMEM((tm, mix), BF16)],
        compiler_params=pltpu.CompilerParams(
            dimension_semantics=("arbitrary",), vmem_limit_bytes=VMEM_LIMIT),
        name="out_proj",
    )(x, *parts, mq, z, mk, mv, w_out)


def _head_gain(g, heads, scale=1.0):
    return jnp.tile(g.astype(F32), heads) * scale


def _memory_kv(mem2, mem_norm_g, w_mem_kv, k_gain):
    d = mem2.shape[1]
    w = MEM_HEADS * HEAD_DIM
    gains = _head_gain(k_gain, MEM_HEADS)[None, :]
    mk, mv = _proj(mem2, mem_norm_g[None, :].astype(F32), w_mem_kv.astype(BF16), gains,
                   [(0, w, "norm", 0), (w, w, "plain", 0)], [BF16, BF16], tm=256, name="mem_kv")
    return mk, mv


def _perm_heads(w, perm, axis):
    shp = w.shape
    n = shp[axis] // HEAD_DIM
    w = w.reshape(shp[:axis] + (n, HEAD_DIM) + shp[axis + 1:])
    w = jnp.take(w, jnp.asarray(perm), axis=axis)
    return w.reshape(shp)


def _even_layer(x2, mem2, batch, seq, norm_g, w_in, qk_g, sinks, mem_norm_g, w_mem_kv, w_out):
    aw, kvw, bw, mw = A_Q_HEADS * HEAD_DIM, A_KV_HEADS * HEAD_DIM, B_HEADS * HEAD_DIM, MEM_HEADS * HEAD_DIM
    mix = aw + bw + mw
    o = np.cumsum([0, aw, kvw, kvw, bw, bw, bw, mw, mix])
    w = jnp.concatenate([
        _perm_heads(w_in[:, o[0]:o[1]], A_PERM, 1),
        w_in[:, o[1]:o[7]],
        _perm_heads(w_in[:, o[7]:o[7] + aw], A_PERM, 1),
        w_in[:, o[7] + aw:],
    ], axis=1).astype(BF16)
    gains = jnp.concatenate([
        _head_gain(qk_g[0], A_Q_HEADS, SCALE * LOG2E), _head_gain(qk_g[1], A_KV_HEADS),
        _head_gain(qk_g[2], B_HEADS, SCALE * LOG2E), _head_gain(qk_g[3], B_HEADS),
        _head_gain(qk_g[4], MEM_HEADS, SCALE)])[None, :]
    go = np.cumsum([0, aw, kvw, bw, bw])
    groups = [
        (int(o[0]), aw, "norm", int(go[0])), (int(o[1]), kvw, "norm", int(go[1])),
        (int(o[2]), kvw, "plain", 0),
        (int(o[3]), bw, "norm", int(go[2])), (int(o[4]), bw, "norm", int(go[3])),
        (int(o[5]), bw, "plain", 0),
        (int(o[6]), mw, "norm", int(go[4])), (int(o[7]), mix, "plain", 0)]
    aq, ak, av, bq, bk, bv, mq, z = _proj(
        x2, norm_g[None, :].astype(F32), w, gains, groups, [BF16] * 8, tm=512, name="even_in_proj")
    mk, mv = _memory_kv(mem2, mem_norm_g, w_mem_kv, qk_g[5])
    ya = _swa(aq, ak, av, sinks.astype(F32), batch, seq)
    yb = _moba(bq, bk, bv, _kmean(bk, batch, seq), batch, seq)
    w_o = jnp.concatenate([_perm_heads(w_out[:aw], A_PERM, 0), w_out[aw:]], axis=0).astype(BF16)
    return _out_proj(x2, [ya, yb], mq, z, mk, mv, w_o, seq)


def _odd_layer(x2, mem2, batch, seq, norm_g, w_in, qk_g, b_f, mem_norm_g, w_mem_kv, w_out):
    cw, mw = C_HEADS * HEAD_DIM, MEM_HEADS * HEAD_DIM
    mix = cw + mw
    o = np.cumsum([0, cw, cw, cw, C_HEADS, mw, mix])
    d = w_in.shape[0]
    w = jnp.concatenate([
        w_in[:, o[0]:o[4]], jnp.zeros((d, LANES - C_HEADS), w_in.dtype), w_in[:, o[4]:]],
        axis=1).astype(BF16)
    po = np.cumsum([0, cw, cw, cw, LANES, mw, mix])
    gains = jnp.concatenate([
        _head_gain(qk_g[0], C_HEADS, SCALE * LOG2E), _head_gain(qk_g[1], C_HEADS),
        _head_gain(qk_g[2], MEM_HEADS, SCALE)])[None, :]
    groups = [
        (int(po[0]), cw, "norm", 0), (int(po[1]), cw, "norm", cw), (int(po[2]), cw, "plain", 0),
        (int(po[3]), LANES, "plain", 0), (int(po[4]), mw, "norm", 2 * cw), (int(po[5]), mix, "plain", 0)]
    cq, ck, cv, cf, mq, z = _proj(
        x2, norm_g[None, :].astype(F32), w, gains, groups, [BF16, BF16, BF16, F32, BF16, BF16],
        tm=512, name="odd_in_proj")
    mk, mv = _memory_kv(mem2, mem_norm_g, w_mem_kv, qk_g[3])
    qside, kside = _fox_decay(cf, b_f.astype(F32), batch, seq)
    yc = _fox(cq, qside, ck, kside, cv, batch, seq)
    return _out_proj(x2, [yc], mq, z, mk, mv, w_out.astype(BF16), seq)


def kernel(x, mem, e_norm, e_w_in, e_qk_norm, e_sinks, e_mem_norm, e_w_mem_kv, e_w_out,
           o_norm, o_w_in, o_qk_norm, o_b_f, o_mem_norm, o_w_mem_kv, o_w_out):
    batch, seq, d = x.shape
    x2 = x.reshape(batch * seq, d)
    mem2 = mem.reshape(batch * mem.shape[1], d)
    depth = e_norm.shape[0] + o_norm.shape[0]
    for layer in range(depth):
        i = layer // 2
        if layer % 2 == 0:
            x2 = _even_layer(x2, mem2, batch, seq, e_norm[i], e_w_in[i], e_qk_norm[i], e_sinks[i],
                             e_mem_norm[i], e_w_mem_kv[i], e_w_out[i])
        else:
            x2 = _odd_layer(x2, mem2, batch, seq, o_norm[i], o_w_in[i], o_qk_norm[i], o_b_f[i],
                            o_mem_norm[i], o_w_mem_kv[i], o_w_out[i])
    return x2.reshape(batch, seq, d)
```

```python
import functools

import numpy as np
import jax
import jax.numpy as jnp
from jax import lax
from jax.experimental import pallas as pl
from jax.experimental.pallas import tpu as pltpu

HEAD_DIM = 64
LANES = 128
A_Q_HEADS, A_KV_HEADS, WINDOW = 6, 2, 128
B_HEADS, MOBA_BLOCK, MOBA_TOPK = 6, 256, 3
C_HEADS = 12
MEM_HEADS = 4
N_ALIBI = A_Q_HEADS + B_HEADS
EPS = 1e-6
NEG = -1e30
M_INIT = -1e29
SCALE = HEAD_DIM ** -0.5
LOG2E = 1.4426950408889634
VMEM_LIMIT = 56 * 1024 * 1024

A_PERM = (0, 3, 2, 5, 1, 4)

F32 = jnp.float32
BF16 = jnp.bfloat16

_NT = (((1,), (1,)), ((), ()))


def _alibi_slopes():
    h = np.arange(1, N_ALIBI + 1, dtype=np.float32)
    return (np.float32(2.0) ** (np.float32(-8.0) * h / np.float32(N_ALIBI))).astype(np.float32)


def _split3(x):
    hi = x.astype(BF16)
    r1 = x - hi.astype(F32)
    mid = r1.astype(BF16)
    lo = (r1 - mid.astype(F32)).astype(BF16)
    return hi, mid, lo


def _lane_lo(shape_rows=1):
    return lax.broadcasted_iota(jnp.int32, (shape_rows, LANES), 1) < HEAD_DIM


def _proj_body(x_ref, g_ref, w_ref, gain_ref, *rest, groups, chunk, has_wt):
    wt_ref = rest[0] if has_wt else None
    out_refs = rest[1:] if has_wt else rest
    x = x_ref[...]
    ms = jnp.mean(x * x, axis=-1, keepdims=True)
    h = ((x * lax.rsqrt(ms + EPS)) * g_ref[...]).astype(BF16)
    lo = _lane_lo()
    for out_ref, (col0, width, kind, gain0) in zip(out_refs, groups):
        if kind == "plain_t":
            acc_t = lax.dot_general(wt_ref[col0:col0 + width, :], h, _NT, preferred_element_type=F32)
            out_ref[...] = acc_t.astype(out_ref.dtype)
            continue
        for c0 in range(0, width, chunk):
            cw = min(chunk, width - c0)
            acc = jnp.dot(h, w_ref[:, col0 + c0:col0 + c0 + cw], preferred_element_type=F32)
            if kind == "norm":
                for b in range(cw // LANES):
                    t = acc[:, b * LANES:(b + 1) * LANES]
                    sq = t * t
                    ss_lo = jnp.sum(jnp.where(lo, sq, 0.0), axis=-1, keepdims=True)
                    ss_hi = jnp.sum(jnp.where(lo, 0.0, sq), axis=-1, keepdims=True)
                    msq = jnp.where(lo, ss_lo, ss_hi) * (1.0 / HEAD_DIM)
                    g0 = gain0 + c0 + b * LANES
                    y = (t * lax.rsqrt(msq + EPS)) * gain_ref[:, g0:g0 + LANES]
                    out_ref[:, c0 + b * LANES:c0 + (b + 1) * LANES] = y.astype(out_ref.dtype)
            else:
                out_ref[:, c0:c0 + cw] = acc.astype(out_ref.dtype)


def _proj(x, g, w, gains, groups, out_dtypes, tm, name, wt=None):
    t, d = x.shape
    n = w.shape[1]
    out_shape, out_specs = [], []
    for (_, width, kind, _), dt in zip(groups, out_dtypes):
        if kind == "plain_t":
            out_shape.append(jax.ShapeDtypeStruct((width, t), dt))
            out_specs.append(pl.BlockSpec((width, tm), lambda i: (0, i)))
        else:
            out_shape.append(jax.ShapeDtypeStruct((t, width), dt))
            out_specs.append(pl.BlockSpec((tm, width), lambda i: (i, 0)))
    operands = [x, g, w, gains]
    in_specs = [
        pl.BlockSpec((tm, d), lambda i: (i, 0)),
        pl.BlockSpec((1, d), lambda i: (0, 0)),
        pl.BlockSpec((d, n), lambda i: (0, 0)),
        pl.BlockSpec((1, gains.shape[1]), lambda i: (0, 0)),
    ]
    if wt is not None:
        operands.append(wt)
        in_specs.append(pl.BlockSpec(wt.shape, lambda i: (0, 0)))
    return pl.pallas_call(
        functools.partial(_proj_body, groups=tuple(groups), chunk=512, has_wt=wt is not None),
        grid=(t // tm,),
        in_specs=in_specs,
        out_specs=out_specs,
        out_shape=out_shape,
        compiler_params=pltpu.CompilerParams(
            dimension_semantics=("arbitrary",), vmem_limit_bytes=VMEM_LIMIT),
        name=name,
    )(*operands)


SWA_BLOCKS_PER_STEP = 4


def _swa_side_tables():
    slopes = _alibi_slopes()[:A_Q_HEADS].astype(np.float64) * LOG2E
    r = np.arange(WINDOW, dtype=np.float64)
    qside = np.zeros((A_Q_HEADS, WINDOW, LANES), np.float32)
    for pos, head in enumerate(A_PERM):
        qside[pos, :, 0] = slopes[head]
        qside[pos, :, 3] = -slopes[head] * (r + WINDOW)
    qside = jnp.asarray(qside)
    pieces = jnp.zeros(qside.shape, BF16)
    for base in (0, 3):
        for n, piece in enumerate(_split3(qside[:, :, base])):
            pieces = pieces.at[:, :, base + n].set(piece)
    kside = np.zeros((2 * WINDOW, LANES), np.float32)
    kside[:, 0:3] = np.arange(2 * WINDOW, dtype=np.float32)[:, None]
    kside[:, 3:6] = 1.0
    return pieces.reshape(A_Q_HEADS * WINDOW, LANES), jnp.asarray(kside, BF16)


def _swa_body(sink_ref, q_ref, kp_ref, kc_ref, vp_ref, vc_ref, qs_ref, ks_ref, o_ref):
    i = pl.program_id(1)
    w = WINDOW
    k = jnp.concatenate([kp_ref[...], kc_ref[...]], axis=0)
    v = jnp.concatenate([vp_ref[...], vc_ref[...]], axis=0)
    lo = _lane_lo()
    r_i = lax.broadcasted_iota(jnp.int32, (w, 2 * w), 0)
    c_i = lax.broadcasted_iota(jnp.int32, (w, 2 * w), 1)
    band = (c_i <= r_i + w) & (c_i > r_i)
    band_first = band & ((c_i >= w) | (i > 0))
    sink = jnp.concatenate(
        [jnp.full((w, LANES), sink_ref[head] * LOG2E, F32) for head in A_PERM], axis=0)
    ones = jnp.ones((2 * w, LANES), BF16)
    for n in range(SWA_BLOCKS_PER_STEP):
        qb = q_ref[n * w:(n + 1) * w, :]
        parts = []
        for pos in range(A_Q_HEADS):
            blk = qb[:, (pos // 2) * LANES:(pos // 2 + 1) * LANES]
            parts.append(jnp.where(lo if pos % 2 == 0 else jnp.logical_not(lo), blk,
                                   jnp.zeros_like(blk)))
        qa = jnp.concatenate([jnp.concatenate(parts, axis=0), qs_ref[...]], axis=1)
        ka = jnp.concatenate([k[n * w:(n + 2) * w, :], ks_ref[...]], axis=1)
        va = jnp.concatenate([v[n * w:(n + 2) * w, :], ones], axis=1)
        s = lax.dot_general(qa, ka, _NT, preferred_element_type=F32)
        mask = band_first if n == 0 else band
        s = jnp.where(jnp.concatenate([mask] * A_Q_HEADS, axis=0), s, NEG)
        m = jnp.maximum(jnp.max(s, axis=-1, keepdims=True), sink)
        p = jnp.exp2(s - _lane_tile(m, 2 * w // LANES)).astype(BF16)
        pv = jnp.dot(p, va, preferred_element_type=F32)
        out = pv[:, :LANES] / (pv[:, LANES:] + jnp.exp2(sink - m))
        for j in range(A_Q_HEADS // 2):
            o_ref[n * w:(n + 1) * w, j * LANES:(j + 1) * LANES] = jnp.where(
                lo, out[2 * j * w:(2 * j + 1) * w], out[(2 * j + 1) * w:(2 * j + 2) * w]
            ).astype(o_ref.dtype)


def _swa(aq, ak, av, sinks, batch, seq):
    t = aq.shape[0]
    tq = SWA_BLOCKS_PER_STEP * WINDOW
    nq = seq // tq
    nb = seq // WINDOW
    qside, kside = _swa_side_tables()
    prev_map = lambda b, i: (jnp.maximum(b * nb + SWA_BLOCKS_PER_STEP * i - 1, b * nb), 0)
    cur_map = lambda b, i: (b * nq + i, 0)
    const = lambda b, i: (0, 0)
    return pl.pallas_call(
        _swa_body,
        grid=(batch, nq),
        in_specs=[
            pl.BlockSpec(memory_space=pltpu.SMEM),
            pl.BlockSpec((tq, A_Q_HEADS * HEAD_DIM), cur_map),
            pl.BlockSpec((WINDOW, LANES), prev_map),
            pl.BlockSpec((tq, LANES), cur_map),
            pl.BlockSpec((WINDOW, LANES), prev_map),
            pl.BlockSpec((tq, LANES), cur_map),
            pl.BlockSpec(qside.shape, const),
            pl.BlockSpec(kside.shape, const),
        ],
        out_specs=pl.BlockSpec((tq, A_Q_HEADS * HEAD_DIM), cur_map),
        out_shape=jax.ShapeDtypeStruct((t, A_Q_HEADS * HEAD_DIM), BF16),
        compiler_params=pltpu.CompilerParams(
            dimension_semantics=("arbitrary", "arbitrary"), vmem_limit_bytes=VMEM_LIMIT),
        name="swa_attn",
    )(sinks, aq, ak, ak, av, av, qside, kside)


ONES_ROWS = 16


def _flash_scratch(tq, tk):
    r = 2 * tq
    return [
        pltpu.VMEM((2 * LANES, r), BF16),
        pltpu.VMEM((2, tk, r), F32),
        pltpu.VMEM((2, tk, r), BF16),
        pltpu.VMEM((2, 8, r), F32),
        pltpu.VMEM((8, r), F32),
        pltpu.VMEM((LANES + ONES_ROWS, r), F32),
    ]


def _lane_tile(x, n):
    return jnp.concatenate([x] * n, axis=1)


def _flash_loop(scratch, k_ref, ks_ref, vt_ref, o_ref, *, tq, tk, diag, n_past, diag_mask):
    qat_ref, s_ref, p_ref, al_ref, m_ref, acc_ref = scratch

    def stage_a(slot, c, mask=None):
        off = pl.multiple_of(c * tk, tk)
        ka = jnp.concatenate([k_ref[pl.ds(off, tk), :], ks_ref[pl.ds(off, tk), :]], axis=1)
        s = jnp.dot(ka, qat_ref[...], preferred_element_type=F32)
        if mask is not None:
            s = jnp.where(mask, s, NEG)
        s_ref[slot] = s

    def stage_b(slot):
        s = s_ref[slot]
        m_prev = m_ref[...]
        m_next = jnp.maximum(m_prev, jnp.max(s, axis=0, keepdims=True))
        al_ref[slot] = jnp.exp2(m_prev - m_next)
        p_ref[slot] = jnp.exp2(s - m_next[0:1, :]).astype(BF16)
        m_ref[...] = m_next

    def stage_c(slot, c):
        off = pl.multiple_of(c * tk, tk)
        vat = jnp.concatenate([vt_ref[:, pl.ds(off, tk)], jnp.ones((ONES_ROWS, tk), BF16)], axis=0)
        pv = jnp.dot(vat, p_ref[slot], preferred_element_type=F32)
        acc_ref[...] = al_ref[slot][0:1, :] * acc_ref[...] + pv

    m_ref[...] = jnp.full(m_ref.shape, M_INIT, F32)
    acc_ref[...] = jnp.zeros(acc_ref.shape, F32)
    stage_a(0, diag, diag_mask)
    stage_b(0)
    stage_a(1, 0)

    def pair(u, carry):
        t = 2 * u + 1
        stage_c(0, jnp.where(u == 0, diag, t - 2))
        stage_b(1)
        stage_a(0, t)
        stage_c(1, t - 1)
        stage_b(0)
        stage_a(1, t + 1)
        return carry

    lax.fori_loop(0, n_past // 2, pair, 0)
    odd = (n_past % 2) == 1

    @pl.when(odd)
    def _():
        stage_c(0, jnp.where(n_past == 1, diag, n_past - 2))
        stage_b(1)
        stage_c(1, n_past - 1)

    @pl.when(jnp.logical_not(odd))
    def _():
        stage_c(0, jnp.where(n_past == 0, diag, n_past - 1))

    acc = acc_ref[...]
    out_t = acc[:LANES, :] / acc[LANES:LANES + 1, :]
    head0 = lax.broadcasted_iota(jnp.int32, (LANES, 1), 0) < HEAD_DIM
    o_ref[...] = jnp.where(head0, out_t[:, :tq], out_t[:, tq:]).T.astype(o_ref.dtype)


def _stack_heads_t(qat_ref, q, side_t_of_head, tq):
    q_t = q.astype(F32).T
    head0 = lax.broadcasted_iota(jnp.int32, (LANES, 1), 0) < HEAD_DIM
    for h in range(2):
        qm = jnp.where(head0 == (h == 0), q_t, 0.0)
        qat_ref[:, h * tq:(h + 1) * tq] = jnp.concatenate(
            [qm, side_t_of_head(h)], axis=0).astype(BF16)


FOX_SIDE_LANES = 6


def _fox_body(q_ref, qs_ref, k_ref, ks_ref, vt_ref, o_ref, *scratch, tq):
    i = pl.program_id(2)
    qs_t = qs_ref[...].astype(F32).T
    side_row = lax.broadcasted_iota(jnp.int32, (LANES, 1), 0)

    def side(h):
        side0 = (2 * pl.program_id(1) + h) * FOX_SIDE_LANES
        return jnp.where((side_row >= side0) & (side_row < side0 + FOX_SIDE_LANES), qs_t, 0.0)

    _stack_heads_t(scratch[0], q_ref[...], side, tq)
    key = lax.broadcasted_iota(jnp.int32, (tq, tq), 0)
    qry = lax.broadcasted_iota(jnp.int32, (tq, tq), 1)
    causal = key <= qry
    _flash_loop(scratch, k_ref, ks_ref, vt_ref, o_ref, tq=tq, tk=tq, diag=i, n_past=i,
                diag_mask=jnp.concatenate([causal, causal], axis=1))


def _fox(cq, qside, ck, kside, cv_t, batch, seq, tq=512):
    t, c = cq.shape
    nq = seq // tq
    q_map = lambda b, p, i: (b * nq + i, p)
    kv_map = lambda b, p, i: (b, p)
    return pl.pallas_call(
        functools.partial(_fox_body, tq=tq),
        grid=(batch, c // LANES, nq),
        in_specs=[
            pl.BlockSpec((tq, LANES), q_map),
            pl.BlockSpec((tq, LANES), lambda b, p, i: (b * nq + i, 0)),
            pl.BlockSpec((seq, LANES), kv_map),
            pl.BlockSpec((seq, LANES), lambda b, p, i: (b, 0)),
            pl.BlockSpec((LANES, seq), lambda b, p, i: (p, b)),
        ],
        out_specs=pl.BlockSpec((tq, LANES), q_map),
        out_shape=jax.ShapeDtypeStruct((t, c), BF16),
        scratch_shapes=_flash_scratch(tq, tq),
        compiler_params=pltpu.CompilerParams(
            dimension_semantics=("arbitrary", "arbitrary", "arbitrary"),
            vmem_limit_bytes=VMEM_LIMIT),
        name="fox_attn",
    )(cq, qside, ck, kside, cv_t)


def _fox_decay_body(cf_ref, bf_ref, tri_ref, pq_ref, pk_ref, cq_ref, ck_ref,
                    qs_ref, ks_ref, carry_ref):
    @pl.when(pl.program_id(1) == 0)
    def _():
        carry_ref[...] = jnp.zeros(carry_ref.shape, F32)

    x = cf_ref[...] + bf_ref[...]
    logf = -(jnp.maximum(-x, 0.0) + jnp.log1p(jnp.exp(-jnp.abs(x))))
    tri = tri_ref[...]
    c = carry_ref[...]
    for piece in _split3(logf):
        c = c + jnp.dot(tri, piece, preferred_element_type=F32)
    carry_ref[...] = c[c.shape[0] - 1:, :]
    qs = cq_ref[...]
    ks = ck_ref[...]
    for n, piece in enumerate(_split3(c * LOG2E)):
        qs = qs + jnp.dot(piece, pq_ref[n], preferred_element_type=F32)
        ks = ks + jnp.dot(piece, pk_ref[n], preferred_element_type=F32)
    qs_ref[...] = qs.astype(BF16)
    ks_ref[...] = ks.astype(BF16)


def _fox_side_tables():
    width = LANES
    assert C_HEADS * FOX_SIDE_LANES <= LANES
    pq = np.zeros((3, LANES, width), np.float32)
    pk = np.zeros((3, LANES, width), np.float32)
    cq = np.zeros((1, width), np.float32)
    ck = np.zeros((1, width), np.float32)
    for head in range(C_HEADS):
        base = head * FOX_SIDE_LANES
        for n in range(3):
            pq[n, head, base + n] = 1.0
            ck[0, base + n] = 1.0
            cq[0, base + 3 + n] = 1.0
            pk[n, head, base + 3 + n] = -1.0
    return (jnp.asarray(pq, BF16), jnp.asarray(pk, BF16), jnp.asarray(cq), jnp.asarray(ck))


def _fox_decay(cf, b_f, batch, seq, rows=256):
    t = cf.shape[0]
    width = LANES
    nr = seq // rows
    bf_row = jnp.zeros((1, LANES), F32).at[0, :C_HEADS].set(b_f)
    tri = jnp.asarray(np.tril(np.ones((rows, rows), np.float32)), BF16)
    pq, pk, cq, ck = _fox_side_tables()
    row_map = lambda b, r: (b * nr + r, 0)
    const2 = lambda b, r: (0, 0)
    const3 = lambda b, r: (0, 0, 0)
    return pl.pallas_call(
        _fox_decay_body,
        grid=(batch, nr),
        in_specs=[
            pl.BlockSpec((rows, LANES), row_map),
            pl.BlockSpec((1, LANES), const2),
            pl.BlockSpec((rows, rows), const2),
            pl.BlockSpec((3, LANES, width), const3),
            pl.BlockSpec((3, LANES, width), const3),
            pl.BlockSpec((1, width), const2),
            pl.BlockSpec((1, width), const2),
        ],
        out_specs=[pl.BlockSpec((rows, width), row_map), pl.BlockSpec((rows, width), row_map)],
        out_shape=[jax.ShapeDtypeStruct((t, width), BF16), jax.ShapeDtypeStruct((t, width), BF16)],
        scratch_shapes=[pltpu.VMEM((1, LANES), F32)],
        compiler_params=pltpu.CompilerParams(
            dimension_semantics=("arbitrary", "arbitrary"), vmem_limit_bytes=VMEM_LIMIT),
        name="fox_decay",
    )(cf, bf_row, tri, pq, pk, cq, ck)


MOBA_MAX_BLOCKS = 32
_L_POS, _L_BLK, _L_ONE = MOBA_MAX_BLOCKS, MOBA_MAX_BLOCKS + 3, MOBA_MAX_BLOCKS + 6


def _moba_ktab(seq):
    tab = np.zeros((seq, LANES), np.float32)
    s = np.arange(seq)
    tab[s, s // MOBA_BLOCK] = 1.0
    for n in range(3):
        tab[:, _L_POS + n] = s % MOBA_BLOCK
        tab[:, _L_BLK + n] = s // MOBA_BLOCK
        tab[:, _L_ONE + n] = 1.0
    return jnp.asarray(tab, BF16)


def _moba_body(slope_ref, q_ref, k_ref, kt_ref, vt_ref, km_ref, o_ref, *scratch, tk):
    pair = pl.program_id(1)
    i = pl.program_id(2)
    tq = MOBA_BLOCK
    q_t = q_ref[...].astype(F32).T
    head0 = lax.broadcasted_iota(jnp.int32, (LANES, 1), 0) < HEAD_DIM
    km_pieces = _split3(km_ref[0])
    blk = lax.broadcasted_iota(jnp.int32, (MOBA_MAX_BLOCKS, tq), 0)
    blkf = blk.astype(F32)
    past = blk < i
    r16 = lax.broadcasted_iota(jnp.int32, (16, tq), 0)
    t_abs = (i * tq + lax.broadcasted_iota(jnp.int32, (1, tq), 1)).astype(F32)

    def side(h):
        qm = jnp.where(head0 == (h == 0), q_t, 0.0).astype(BF16)
        gate = jnp.zeros((MOBA_MAX_BLOCKS, tq), F32)
        for piece in km_pieces:
            gate = gate + jnp.dot(piece, qm, preferred_element_type=F32)
        g = jnp.where(past, gate, NEG)
        sel = jnp.zeros((MOBA_MAX_BLOCKS, tq), jnp.bool_)
        for _ in range(MOBA_TOPK):
            mx = jnp.max(g, axis=0, keepdims=True)
            first = jnp.min(jnp.where(g == mx, blkf, 1e9), axis=0, keepdims=True)
            pick = blkf == first
            sel = sel | pick
            g = jnp.where(pick, -jnp.inf, g)
        sel = (sel & past) | (blk == i)
        selbias = jnp.where(sel, 0.0, NEG)
        slope = jnp.full((1, tq), slope_ref[2 * pair + h], F32)
        rows = _split3(slope) + _split3(slope * MOBA_BLOCK) + _split3(-(slope * t_abs))
        extra = jnp.zeros((16, tq), F32)
        for n, rv in enumerate(rows):
            extra = jnp.where(r16 == n, rv.astype(F32), extra)
        return jnp.concatenate(
            [selbias, extra, jnp.zeros((LANES - MOBA_MAX_BLOCKS - 16, tq), F32)], axis=0)

    _stack_heads_t(scratch[0], q_ref[...], side, tq)
    r = tk // tq
    n_main = i // r
    key = lax.broadcasted_iota(jnp.int32, (tk, tq), 0)
    qry = lax.broadcasted_iota(jnp.int32, (tk, tq), 1)
    causal = (key - qry) <= (i - n_main * r) * tq
    _flash_loop(scratch, k_ref, kt_ref, vt_ref, o_ref, tq=tq, tk=tk, diag=n_main, n_past=n_main,
                diag_mask=jnp.concatenate([causal, causal], axis=1))


def _moba(bq, bk, bv_t, kmean, batch, seq, tk=512):
    t, c = bq.shape
    tq = MOBA_BLOCK
    nq = seq // tq
    assert nq <= MOBA_MAX_BLOCKS and seq % tk == 0
    slopes = jnp.asarray(_alibi_slopes()[A_Q_HEADS:] * np.float32(LOG2E))
    ktab = _moba_ktab(seq)
    q_map = lambda b, p, i: (b * nq + i, p)
    kv_map = lambda b, p, i: (b, p)
    return pl.pallas_call(
        functools.partial(_moba_body, tk=tk),
        grid=(batch, c // LANES, nq),
        in_specs=[
            pl.BlockSpec(memory_space=pltpu.SMEM),
            pl.BlockSpec((tq, LANES), q_map),
            pl.BlockSpec((seq, LANES), kv_map),
            pl.BlockSpec((seq, LANES), lambda b, p, i: (0, 0)),
            pl.BlockSpec((LANES, seq), lambda b, p, i: (p, b)),
            pl.BlockSpec((1, MOBA_MAX_BLOCKS, LANES), lambda b, p, i: (b, 0, p)),
        ],
        out_specs=pl.BlockSpec((tq, LANES), q_map),
        out_shape=jax.ShapeDtypeStruct((t, c), BF16),
        scratch_shapes=_flash_scratch(tq, tk),
        compiler_params=pltpu.CompilerParams(
            dimension_semantics=("arbitrary", "arbitrary", "arbitrary"),
            vmem_limit_bytes=VMEM_LIMIT),
        name="moba_attn",
    )(slopes, bq, bk, ktab, bv_t, kmean)


def _kmean_body(k_ref, o_ref, *, nblk):
    for j in range(nblk):
        kb = k_ref[j * MOBA_BLOCK:(j + 1) * MOBA_BLOCK, :].astype(F32)
        o_ref[0, j:j + 1, :] = jnp.mean(kb, axis=0, keepdims=True)
    if nblk < MOBA_MAX_BLOCKS:
        o_ref[0, nblk:, :] = jnp.zeros((MOBA_MAX_BLOCKS - nblk, o_ref.shape[2]), F32)


def _kmean(bk, batch, seq):
    c = bk.shape[1]
    nblk = seq // MOBA_BLOCK
    return pl.pallas_call(
        functools.partial(_kmean_body, nblk=nblk),
        grid=(batch,),
        in_specs=[pl.BlockSpec((seq, c), lambda b: (b, 0))],
        out_specs=pl.BlockSpec((1, MOBA_MAX_BLOCKS, c), lambda b: (b, 0, 0)),
        out_shape=jax.ShapeDtypeStruct((batch, MOBA_MAX_BLOCKS, c), F32),
        compiler_params=pltpu.CompilerParams(
            dimension_semantics=("arbitrary",), vmem_limit_bytes=VMEM_LIMIT),
        name="moba_kmean",
    )(bk)


def _out_body(*refs, n_parts):
    x_ref = refs[0]
    part_refs = refs[1:1 + n_parts]
    mq_ref, z_ref, mk_ref, mv_ref, w_ref, o_ref, y_ref = refs[1 + n_parts:]
    lo = _lane_lo()
    col = 0
    for pr in part_refs:
        w = pr.shape[1]
        zz = z_ref[:, col:col + w].astype(F32)
        y_ref[:, col:col + w] = (pr[...].astype(F32) * (zz / (1.0 + jnp.exp(-zz)))).astype(BF16)
        col += w
    for pp in range(MEM_HEADS // 2):
        sl = slice(pp * LANES, (pp + 1) * LANES)
        q = mq_ref[:, sl]
        k = mk_ref[:, sl]
        v = mv_ref[:, sl]
        halves = []
        for hh in range(2):
            qm = jnp.where(lo if hh == 0 else jnp.logical_not(lo), q, jnp.zeros_like(q))
            s = lax.dot_general(qm, k, _NT, preferred_element_type=F32)
            e = jnp.exp(s - jnp.max(s, axis=-1, keepdims=True))
            p = (e / jnp.sum(e, axis=-1, keepdims=True)).astype(BF16)
            halves.append(jnp.dot(p, v, preferred_element_type=F32))
        ym = jnp.where(lo, halves[0], halves[1])
        zz = z_ref[:, col + pp * LANES:col + (pp + 1) * LANES].astype(F32)
        y_ref[:, col + pp * LANES:col + (pp + 1) * LANES] = (ym * (zz / (1.0 + jnp.exp(-zz)))).astype(BF16)
    o_ref[...] = x_ref[...] + jnp.dot(y_ref[...], w_ref[...], preferred_element_type=F32)


def _out_proj(x, parts, mq, z, mk, mv, w_out, seq, tm=512):
    t, d = x.shape
    mem_len = mk.shape[0] // (t // seq)
    per_b = seq // tm
    row_map = lambda i: (i, 0)
    mem_map = lambda i: (i // per_b, 0)
    mix = z.shape[1]
    return pl.pallas_call(
        functools.partial(_out_body, n_parts=len(parts)),
        grid=(t // tm,),
        in_specs=[pl.BlockSpec((tm, d), row_map)]
        + [pl.BlockSpec((tm, p.shape[1]), row_map) for p in parts]
        + [
            pl.BlockSpec((tm, mq.shape[1]), row_map),
            pl.BlockSpec((tm, mix), row_map),
            pl.BlockSpec((mem_len, mk.shape[1]), mem_map),
            pl.BlockSpec((mem_len, mv.shape[1]), mem_map),
            pl.BlockSpec((mix, d), lambda i: (0, 0)),
        ],
        out_specs=pl.BlockSpec((tm, d), row_map),
        out_shape=jax.ShapeDtypeStruct((t, d), F32),
        scratch_shapes=[pltpu.VMEM((tm, mix), BF16)],
        compiler_params=pltpu.CompilerParams(
            dimension_semantics=("arbitrary",), vmem_limit_bytes=VMEM_LIMIT),
        name="out_proj",
    )(x, *parts, mq, z, mk, mv, w_out)


def _head_gain(g, heads, scale=1.0):
    return jnp.tile(g.astype(F32), heads) * scale


def _memory_kv(mem2, mem_norm_g, w_mem_kv, k_gain):
    d = mem2.shape[1]
    w = MEM_HEADS * HEAD_DIM
    gains = _head_gain(k_gain, MEM_HEADS)[None, :]
    mk, mv = _proj(mem2, mem_norm_g[None, :].astype(F32), w_mem_kv.astype(BF16), gains,
                   [(0, w, "norm", 0), (w, w, "plain", 0)], [BF16, BF16], tm=256, name="mem_kv")
    return mk, mv


def _perm_heads(w, perm, axis):
    shp = w.shape
    n = shp[axis] // HEAD_DIM
    w = w.reshape(shp[:axis] + (n, HEAD_DIM) + shp[axis + 1:])
    w = jnp.take(w, jnp.asarray(perm), axis=axis)
    return w.reshape(shp)


def _even_layer(x2, mem2, batch, seq, norm_g, w_in, qk_g, sinks, mem_norm_g, w_mem_kv, w_out):
    aw, kvw, bw, mw = A_Q_HEADS * HEAD_DIM, A_KV_HEADS * HEAD_DIM, B_HEADS * HEAD_DIM, MEM_HEADS * HEAD_DIM
    mix = aw + bw + mw
    o = np.cumsum([0, aw, kvw, kvw, bw, bw, bw, mw, mix])
    w = jnp.concatenate([
        _perm_heads(w_in[:, o[0]:o[1]], A_PERM, 1),
        w_in[:, o[1]:o[7]],
        _perm_heads(w_in[:, o[7]:o[7] + aw], A_PERM, 1),
        w_in[:, o[7] + aw:],
    ], axis=1).astype(BF16)
    gains = jnp.concatenate([
        _head_gain(qk_g[0], A_Q_HEADS, SCALE * LOG2E), _head_gain(qk_g[1], A_KV_HEADS),
        _head_gain(qk_g[2], B_HEADS, SCALE * LOG2E), _head_gain(qk_g[3], B_HEADS),
        _head_gain(qk_g[4], MEM_HEADS, SCALE)])[None, :]
    go = np.cumsum([0, aw, kvw, bw, bw])
    groups = [
        (int(o[0]), aw, "norm", int(go[0])), (int(o[1]), kvw, "norm", int(go[1])),
        (int(o[2]), kvw, "plain", 0),
        (int(o[3]), bw, "norm", int(go[2])), (int(o[4]), bw, "norm", int(go[3])),
        (0, bw, "plain_t", 0),
        (int(o[6]), mw, "norm", int(go[4])), (int(o[7]), mix, "plain", 0)]
    wt = w_in[:, o[5]:o[6]].T.astype(BF16)
    aq, ak, av, bq, bk, bv_t, mq, z = _proj(
        x2, norm_g[None, :].astype(F32), w, gains, groups, [BF16] * 8, tm=512, name="even_in_proj",
        wt=wt)
    mk, mv = _memory_kv(mem2, mem_norm_g, w_mem_kv, qk_g[5])
    ya = _swa(aq, ak, av, sinks.astype(F32), batch, seq)
    yb = _moba(bq, bk, bv_t, _kmean(bk, batch, seq), batch, seq)
    w_o = jnp.concatenate([_perm_heads(w_out[:aw], A_PERM, 0), w_out[aw:]], axis=0).astype(BF16)
    return _out_proj(x2, [ya, yb], mq, z, mk, mv, w_o, seq)


def _odd_layer(x2, mem2, batch, seq, norm_g, w_in, qk_g, b_f, mem_norm_g, w_mem_kv, w_out):
    cw, mw = C_HEADS * HEAD_DIM, MEM_HEADS * HEAD_DIM
    mix = cw + mw
    o = np.cumsum([0, cw, cw, cw, C_HEADS, mw, mix])
    d = w_in.shape[0]
    w = jnp.concatenate([
        w_in[:, o[0]:o[4]], jnp.zeros((d, LANES - C_HEADS), w_in.dtype), w_in[:, o[4]:]],
        axis=1).astype(BF16)
    po = np.cumsum([0, cw, cw, cw, LANES, mw, mix])
    gains = jnp.concatenate([
        _head_gain(qk_g[0], C_HEADS, SCALE * LOG2E), _head_gain(qk_g[1], C_HEADS),
        _head_gain(qk_g[2], MEM_HEADS, SCALE)])[None, :]
    groups = [
        (int(po[0]), cw, "norm", 0), (int(po[1]), cw, "norm", cw), (0, cw, "plain_t", 0),
        (int(po[3]), LANES, "plain", 0), (int(po[4]), mw, "norm", 2 * cw), (int(po[5]), mix, "plain", 0)]
    wt = w_in[:, o[2]:o[3]].T.astype(BF16)
    cq, ck, cv_t, cf, mq, z = _proj(
        x2, norm_g[None, :].astype(F32), w, gains, groups, [BF16, BF16, BF16, F32, BF16, BF16],
        tm=512, name="odd_in_proj", wt=wt)
    mk, mv = _memory_kv(mem2, mem_norm_g, w_mem_kv, qk_g[3])
    qside, kside = _fox_decay(cf, b_f.astype(F32), batch, seq)
    yc = _fox(cq, qside, ck, kside, cv_t, batch, seq)
    return _out_proj(x2, [yc], mq, z, mk, mv, w_out.astype(BF16), seq)


def kernel(x, mem, e_norm, e_w_in, e_qk_norm, e_sinks, e_mem_norm, e_w_mem_kv, e_w_out,
           o_norm, o_w_in, o_qk_norm, o_b_f, o_mem_norm, o_w_mem_kv, o_w_out):
    batch, seq, d = x.shape
    x2 = x.reshape(batch * seq, d)
    mem2 = mem.reshape(batch * mem.shape[1], d)
    depth = e_norm.shape[0] + o_norm.shape[0]
    for layer in range(depth):
        i = layer // 2
        if layer % 2 == 0:
            x2 = _even_layer(x2, mem2, batch, seq, e_norm[i], e_w_in[i], e_qk_norm[i], e_sinks[i],
                             e_mem_norm[i], e_w_mem_kv[i], e_w_out[i])
        else:
            x2 = _odd_layer(x2, mem2, batch, seq, o_norm[i], o_w_in[i], o_qk_norm[i], o_b_f[i],
                            o_mem_norm[i], o_w_mem_kv[i], o_w_out[i])
    return x2.reshape(batch, seq, d)
```
